```python
import math
import jax
import jax.numpy as jnp
from jax import lax
import numpy as np

D_MODEL = 1024
BATCH = 4
SEQ = 8192
DEPTH = 2
DEC_BATCH = 32
DEC_SEQ = 1
PAST_LEN = 16384
PAGE_SIZE = 128

HEAD_DIM = 64
SCALE = HEAD_DIM ** -0.5
A_HEADS = 4
F_HEADS = 8
S_HEADS = 8
S_HEADDIM = 64
S_GROUPS = 2
HPG = S_HEADS // S_GROUPS
D_STATE = 64
CONV_W = 4
SSD_CHUNK = 128
Q_BLOCK = 128
ROT_DIM = HEAD_DIM // 4
ROPE_THETA = 500000.0
A_WIDTH = A_HEADS * 2 * HEAD_DIM
F_WIDTH = F_HEADS * HEAD_DIM
S_WIDTH = S_HEADS * S_HEADDIM
CONV_DIM = S_WIDTH + 2 * S_GROUPS * D_STATE
N_BRANCH = 3
IN_SPLITS = (A_WIDTH, A_WIDTH, A_WIDTH, F_WIDTH, F_WIDTH, F_WIDTH, F_HEADS,
             S_WIDTH, CONV_DIM, S_HEADS, N_BRANCH * D_MODEL)
IN_COLS = sum(IN_SPLITS)
N_EXPERTS = 64
N_EXPERT_GROUPS = 8
TOPK_GROUPS = 4
TOP_K = 8
D_EXPERT = 256
D_SHARED = 256
ROUTED_SCALE = 2.5
MOE_BLOCK = 128
EPS = 1e-6
NEG = -1e30

kernel_name = "hybrid_diff_fox_ssd_moe_adaln_step"


def rmsnorm(x, w):
    xf = x.astype(jnp.float32)
    y = xf * lax.rsqrt(jnp.mean(xf * xf, axis=-1, keepdims=True) + EPS)
    return (y * w.astype(jnp.float32)).astype(x.dtype)


def adaln(c, w_ada_l, b_ada_l):
    mod = jax.nn.silu(c) @ w_ada_l + b_ada_l
    return jnp.split(mod[:, None, :], 6, axis=-1)


def partial_rope(x, pos):
    half = ROT_DIM // 2
    inv = ROPE_THETA ** (-jnp.arange(half, dtype=jnp.float32) / half)
    ang = pos.astype(jnp.float32)[:, None] * inv[None, :]
    cos = jnp.cos(ang)[:, None, None, :].astype(x.dtype)
    sin = jnp.sin(ang)[:, None, None, :].astype(x.dtype)
    x1, x2, rest = x[..., :half], x[..., half:ROT_DIM], x[..., ROT_DIM:]
    return jnp.concatenate([x1 * cos - x2 * sin, x2 * cos + x1 * sin, rest], axis=-1)


def project_inputs(h, p):
    b, t, _ = h.shape
    cuts = np.cumsum(IN_SPLITS)[:-1].tolist()
    aq, ak, av, fq, fk, fv, ff, sz, sxbc, sdt, gates = jnp.split(h @ p['w_in'], cuts, axis=-1)
    aq = aq.reshape(b, t, A_HEADS, 2, HEAD_DIM)
    ak = ak.reshape(b, t, A_HEADS, 2, HEAD_DIM)
    av = av.reshape(b, t, A_HEADS, 2 * HEAD_DIM)
    fq = fq.reshape(b, t, F_HEADS, HEAD_DIM)
    fk = fk.reshape(b, t, F_HEADS, HEAD_DIM)
    fv = fv.reshape(b, t, F_HEADS, HEAD_DIM)
    logf = jax.nn.log_sigmoid((ff + p['fox_b_f']).astype(jnp.float32))
    gates = gates.reshape(b, t, N_BRANCH, D_MODEL)
    return aq, ak, av, fq, fk, fv, logf, sz, sxbc, sdt, gates


def merge_branches(o_a, o_f, o_s, gates, p):
    g = jax.nn.sigmoid(gates)
    m = (g[..., 0, :] * (o_a @ p['w_branch_a']) + g[..., 1, :] * (o_f @ p['w_branch_f'])
         + g[..., 2, :] * (o_s @ p['w_branch_s']))
    return m @ p['w_out']


def gather_pages(pool, layer, page_table):
    g = pool[layer, page_table]
    return g.reshape((g.shape[0], g.shape[1] * g.shape[2]) + g.shape[3:])


def lambda_diff(lam_params, layer):
    lam_init = 0.8 - 0.6 * math.exp(-0.3 * layer)
    lp = lam_params.astype(jnp.float32)
    lam = jnp.exp(jnp.sum(lp[0] * lp[1])) - jnp.exp(jnp.sum(lp[2] * lp[3])) + lam_init
    return lam, lam_init


def diff_finish(o, subln_w, lam_init):
    o = rmsnorm(o, subln_w) * (1.0 - lam_init)
    return o.reshape(o.shape[:2] + (A_WIDTH,))


def diff_attn_prompt(q, k, v, lam):
    b, s = q.shape[:2]
    nb = s // Q_BLOCK
    qb = jnp.moveaxis(q.reshape(b, nb, Q_BLOCK, A_HEADS, 2, HEAD_DIM), 1, 0)
    kpos = jnp.arange(s)

    def block(args):
        qi, i = args
        qpos = i * Q_BLOCK + jnp.arange(Q_BLOCK)
        sc = jnp.einsum('bqhcd,bkhcd->bhcqk', qi, k).astype(jnp.float32) * SCALE
        sc = jnp.where(kpos[None, :] <= qpos[:, None], sc, NEG)
        pr = jax.nn.softmax(sc, axis=-1)
        a = pr[:, :, 0] - lam * pr[:, :, 1]
        return jnp.einsum('bhqk,bkhe->bqhe', a.astype(v.dtype), v)

    o = lax.map(block, (qb, jnp.arange(nb)))
    return jnp.moveaxis(o, 0, 1).reshape(b, s, A_HEADS, 2 * HEAD_DIM)


def diff_attn_sample(q, k_new, v_new, k_past, v_past, lam):
    t = q.shape[1]
    past = k_past.shape[1]
    s_past = jnp.einsum('bqhcd,bkhcd->bhcqk', q, k_past).astype(jnp.float32) * SCALE
    s_new = jnp.einsum('bqhcd,bkhcd->bhcqk', q, k_new).astype(jnp.float32) * SCALE
    causal = jnp.arange(t)[None, :] <= jnp.arange(t)[:, None]
    s_new = jnp.where(causal, s_new, NEG)
    pr = jax.nn.softmax(jnp.concatenate([s_past, s_new], axis=-1), axis=-1)
    a = (pr[:, :, 0] - lam * pr[:, :, 1]).astype(v_new.dtype)
    return (jnp.einsum('bhqk,bkhe->bqhe', a[..., :past], v_past)
            + jnp.einsum('bhqk,bkhe->bqhe', a[..., past:], v_new))


def fox_attn_prompt(q, k, v, logf):
    b, s = q.shape[:2]
    nb = s // Q_BLOCK
    cum = lax.cumsum(logf, axis=1)
    qb = jnp.moveaxis(q.reshape(b, nb, Q_BLOCK, F_HEADS, HEAD_DIM), 1, 0)
    cq = jnp.moveaxis(cum.reshape(b, nb, Q_BLOCK, F_HEADS), 1, 0)
    ck = jnp.swapaxes(cum, 1, 2)[:, :, None, :]
    kpos = jnp.arange(s)

    def block(args):
        qi, ci, i = args
        qpos = i * Q_BLOCK + jnp.arange(Q_BLOCK)
        sc = jnp.einsum('bqhd,bkhd->bhqk', qi, k).astype(jnp.float32) * SCALE
        sc = sc + jnp.swapaxes(ci, 1, 2)[..., None] - ck
        sc = jnp.where(kpos[None, :] <= qpos[:, None], sc, NEG)
        pr = jax.nn.softmax(sc, axis=-1)
        return jnp.einsum('bhqk,bkhd->bqhd', pr.astype(v.dtype), v)

    o = lax.map(block, (qb, cq, jnp.arange(nb)))
    return jnp.moveaxis(o, 0, 1).reshape(b, s, F_WIDTH)


def fox_attn_sample(q, k_new, v_new, logf_new, k_past, v_past, logf_past):
    b, t = q.shape[:2]
    past = k_past.shape[1]
    suffix = lax.cumsum(logf_past, axis=1, reverse=True) - logf_past
    dnew = jnp.swapaxes(lax.cumsum(logf_new, axis=1), 1, 2)
    s_past = jnp.einsum('bqhd,bkhd->bhqk', q, k_past).astype(jnp.float32) * SCALE
    s_past = s_past + dnew[..., None] + jnp.swapaxes(suffix, 1, 2)[:, :, None, :]
    s_new = jnp.einsum('bqhd,bkhd->bhqk', q, k_new).astype(jnp.float32) * SCALE
    s_new = s_new + dnew[..., None] - dnew[:, :, None, :]
    causal = jnp.arange(t)[None, :] <= jnp.arange(t)[:, None]
    s_new = jnp.where(causal, s_new, NEG)
    pr = jax.nn.softmax(jnp.concatenate([s_past, s_new], axis=-1), axis=-1).astype(v_new.dtype)
    o = (jnp.einsum('bhqk,bkhd->bqhd', pr[..., :past], v_past)
         + jnp.einsum('bhqk,bkhd->bqhd', pr[..., past:], v_new))
    return o.reshape(b, t, F_WIDTH)


def causal_conv(xbc, buf, w, bias):
    t = xbc.shape[1]
    xp = jnp.concatenate([buf.astype(xbc.dtype), xbc], axis=1)
    y = bias
    for j in range(CONV_W):
        y = y + w[j] * xp[:, j:j + t]
    return jax.nn.silu(y), xp[:, -(CONV_W - 1):]


def ssd_inputs(xbc, dt_raw, dt_bias, a_log):
    b, t = xbc.shape[:2]
    xs, bm, cm = jnp.split(xbc, [S_WIDTH, S_WIDTH + S_GROUPS * D_STATE], axis=-1)
    xs = xs.reshape(b, t, S_GROUPS, HPG, S_HEADDIM)
    bm = bm.reshape(b, t, S_GROUPS, D_STATE)
    cm = cm.reshape(b, t, S_GROUPS, D_STATE)
    dt = jax.nn.softplus((dt_raw + dt_bias).astype(jnp.float32)).reshape(b, t, S_GROUPS, HPG)
    a = -jnp.exp(a_log.astype(jnp.float32)).reshape(S_GROUPS, HPG)
    return xs, bm, cm, dt, dt * a


def ssd_chunked(xs, bm, cm, dt, da):
    b, s = xs.shape[:2]
    nc, L = s // SSD_CHUNK, SSD_CHUNK
    xd = (xs * dt[..., None]).reshape(b, nc, L, S_GROUPS, HPG, S_HEADDIM)
    bc = bm.reshape(b, nc, L, S_GROUPS, D_STATE)
    cc = cm.reshape(b, nc, L, S_GROUPS, D_STATE)
    a_cs = lax.cumsum(da.reshape(b, nc, L, S_GROUPS, HPG), axis=2)
    seg = a_cs[:, :, :, None] - a_cs[:, :, None, :]
    causal = jnp.tril(jnp.ones((L, L), bool))[None, None, :, :, None, None]
    decay = jnp.where(causal, jnp.exp(jnp.where(causal, seg, 0.0)), 0.0)
    cb = jnp.einsum('bclgn,bcsgn->bclsg', cc, bc)
    y_diag = jnp.einsum('bclsgr,bcsgrp->bclgrp', cb[..., None] * decay, xd)
    to_end = jnp.exp(a_cs[:, :, -1:] - a_cs)
    states = jnp.einsum('bclgn,bclgr,bclgrp->bcgrpn', bc, to_end, xd)
    chunk_decay = jnp.exp(a_cs[:, :, -1])

    def step(h, inp):
        st, dcy = inp
        return h * dcy[..., None, None] + st, h

    h0 = jnp.zeros((b, S_GROUPS, HPG, S_HEADDIM, D_STATE), jnp.float32)
    h_final, h_prev = lax.scan(step, h0, (jnp.moveaxis(states, 1, 0), jnp.moveaxis(chunk_decay, 1, 0)))
    h_prev = jnp.moveaxis(h_prev, 0, 1)
    y_off = jnp.einsum('bclgn,bcgrpn,bclgr->bclgrp', cc, h_prev, jnp.exp(a_cs))
    return (y_diag + y_off).reshape(b, s, S_GROUPS, HPG, S_HEADDIM), h_final


def ssd_recurrent(xs, bm, cm, dt, da, h0):
    def step(h, inp):
        x_t, b_t, c_t, dt_t, da_t = inp
        h = h * jnp.exp(da_t)[..., None, None] + jnp.einsum('bgrp,bgn->bgrpn', x_t * dt_t[..., None], b_t)
        return h, jnp.einsum('bgrpn,bgn->bgrp', h, c_t)

    seqs = (jnp.moveaxis(xs, 1, 0), jnp.moveaxis(bm, 1, 0), jnp.moveaxis(cm, 1, 0),
            jnp.moveaxis(dt, 1, 0), jnp.moveaxis(da, 1, 0))
    h, ys = lax.scan(step, h0, seqs)
    return jnp.moveaxis(ys, 0, 1), h


def ssd_finish(y, xs, z, d_skip, norm_w):
    b, t = z.shape[:2]
    y = y + xs * d_skip.reshape(S_GROUPS, HPG)[..., None]
    y = y.reshape(b, t, S_WIDTH) * jax.nn.silu(z)
    y = rmsnorm(y.reshape(b, t, S_GROUPS, S_WIDTH // S_GROUPS), norm_w.reshape(S_GROUPS, -1))
    return y.reshape(b, t, S_WIDTH).astype(z.dtype)


def mixer_prompt(h, p, layer):
    b, s, _ = h.shape
    aq, ak, av, fq, fk, fv, logf, sz, sxbc, sdt, gates = project_inputs(h, p)
    pos = jnp.arange(s)
    aq, ak = partial_rope(aq, pos), partial_rope(ak, pos)
    lam, lam_init = lambda_diff(p['diff_lambda'], layer)
    o_a = diff_finish(diff_attn_prompt(aq, ak, av, lam), p['diff_subln'], lam_init)
    o_f = fox_attn_prompt(fq, fk, fv, logf)
    buf0 = jnp.zeros((b, CONV_W - 1, CONV_DIM), h.dtype)
    xbc, conv_state = causal_conv(sxbc, buf0, p['ssm_conv_w'], p['ssm_conv_b'])
    xs, bm, cm, dt, da = ssd_inputs(xbc, sdt, p['ssm_dt_bias'], p['ssm_a_log'])
    y, h_final = ssd_chunked(xs, bm, cm, dt, da)
    o_s = ssd_finish(y, xs, sz, p['ssm_d'], p['ssm_norm'])
    out = merge_branches(o_a, o_f, o_s, gates, p)
    return out, (ak, av, fk, fv, logf, h_final.reshape(b, S_HEADS, S_HEADDIM, D_STATE), conv_state)


def mixer_sample(h, p, layer, page_table, k_a_pool, v_a_pool, k_f_pool, v_f_pool, logf_pool, ssm_h, conv_buf):
    b, t, _ = h.shape
    aq, ak, av, fq, fk, fv, logf, sz, sxbc, sdt, gates = project_inputs(h, p)
    pos = page_table.shape[1] * PAGE_SIZE + jnp.arange(t)
    aq, ak = partial_rope(aq, pos), partial_rope(ak, pos)
    lam, lam_init = lambda_diff(p['diff_lambda'], layer)
    o_a = diff_attn_sample(aq, ak, av, gather_pages(k_a_pool, layer, page_table),
                           gather_pages(v_a_pool, layer, page_table), lam)
    o_a = diff_finish(o_a, p['diff_subln'], lam_init)
    o_f = fox_attn_sample(fq, fk, fv, logf, gather_pages(k_f_pool, layer, page_table),
                          gather_pages(v_f_pool, layer, page_table),
                          gather_pages(logf_pool, layer, page_table).astype(jnp.float32))
    xbc, conv_state = causal_conv(sxbc, conv_buf, p['ssm_conv_w'], p['ssm_conv_b'])
    xs, bm, cm, dt, da = ssd_inputs(xbc, sdt, p['ssm_dt_bias'], p['ssm_a_log'])
    h0 = ssm_h.reshape(b, S_GROUPS, HPG, S_HEADDIM, D_STATE).astype(jnp.float32)
    y, h_new = ssd_recurrent(xs, bm, cm, dt, da, h0)
    o_s = ssd_finish(y, xs, sz, p['ssm_d'], p['ssm_norm'])
    out = merge_branches(o_a, o_f, o_s, gates, p)
    return out, (ak, av, fk, fv, logf, h_new.reshape(b, S_HEADS, S_HEADDIM, D_STATE), conv_state)


def route(xf, w_r, b_r):
    scores = jax.nn.sigmoid((xf @ w_r).astype(jnp.float32))
    sel = scores + b_r.astype(jnp.float32)
    per_group = N_EXPERTS // N_EXPERT_GROUPS
    grp_score = lax.top_k(sel.reshape(-1, N_EXPERT_GROUPS, per_group), 2)[0].sum(-1)
    _, gidx = lax.top_k(grp_score, TOPK_GROUPS)
    gmask = jnp.any(gidx[..., None] == jnp.arange(N_EXPERT_GROUPS), axis=-2)
    sel = jnp.where(jnp.repeat(gmask, per_group, axis=-1), sel, NEG)
    _, idx = lax.top_k(sel, TOP_K)
    w = jnp.take_along_axis(scores, idx, axis=-1)
    return idx, w / jnp.sum(w, axis=-1, keepdims=True) * ROUTED_SCALE


def dispatch_experts(xf, idx, wts, w_gate, w_up, w_down):
    n_tok, d = xf.shape
    n_rows = n_tok * TOP_K
    n_blocks = -(-n_rows // MOE_BLOCK) + N_EXPERTS
    flat_e = idx.reshape(-1)
    order = jnp.argsort(flat_e)
    sorted_e = flat_e[order]
    counts = jnp.bincount(flat_e, length=N_EXPERTS)
    padded = (counts + MOE_BLOCK - 1) // MOE_BLOCK * MOE_BLOCK
    start = jnp.cumsum(counts) - counts
    pad_end = jnp.cumsum(padded)
    pad_start = pad_end - padded
    dest = pad_start[sorted_e] + jnp.arange(n_rows) - start[sorted_e]
    slot_tok = jnp.full((n_blocks * MOE_BLOCK,), n_tok, jnp.int32).at[dest].set((order // TOP_K).astype(jnp.int32))
    slot_w = jnp.zeros((n_blocks * MOE_BLOCK,), wts.dtype).at[dest].set(wts.reshape(-1)[order])
    block_e = jnp.minimum(jnp.searchsorted(pad_end, jnp.arange(n_blocks) * MOE_BLOCK, side='right'), N_EXPERTS - 1)
    x_pad = jnp.concatenate([xf, jnp.zeros((1, d), xf.dtype)], axis=0)

    def block(args):
        tok, wt, e = args
        xb = x_pad[tok]
        hb = jax.nn.silu(xb @ w_gate[e]) * (xb @ w_up[e])
        return (hb @ w_down[e]) * wt[:, None].astype(xb.dtype)

    y = lax.map(block, (slot_tok.reshape(n_blocks, MOE_BLOCK), slot_w.reshape(n_blocks, MOE_BLOCK), block_e))
    return jax.ops.segment_sum(y.reshape(-1, d), slot_tok, num_segments=n_tok + 1)[:n_tok]


def moe_ffn(h, p):
    b, t, d = h.shape
    xf = h.reshape(b * t, d)
    idx, wts = route(xf, p['router_w'], p['router_bias'])
    routed = dispatch_experts(xf, idx, wts, p['exp_w_gate'], p['exp_w_up'], p['exp_w_down'])
    shared = (jax.nn.silu(xf @ p['shared_w_gate']) * (xf @ p['shared_w_up'])) @ p['shared_w_down']
    return (routed + shared).reshape(b, t, d)


def residual_layer(x, mod, norm_mix_l, norm_ffn_l, mixer, p):
    shift1, scale1, gate1, shift2, scale2, gate2 = mod
    h = rmsnorm(x, norm_mix_l) * (1.0 + scale1) + shift1
    out, states = mixer(h)
    x = x + gate1 * out
    h = rmsnorm(x, norm_ffn_l) * (1.0 + scale2) + shift2
    x = x + gate2 * moe_ffn(h, p)
    return x, states


def setup_inputs(seed: int = 0) -> dict:
    key = jax.random.key(seed)
    ks = iter(jax.random.split(key, 48))
    nrm = lambda shape, s=1.0: jax.random.normal(next(ks), shape, jnp.float32) * s
    n_pages = PAST_LEN // PAGE_SIZE
    n_used = DEC_BATCH * n_pages
    n_phys = n_used + max(1, n_used // 4)
    page_table = jax.random.permutation(next(ks), n_phys)[:n_used].reshape(DEC_BATCH, n_pages).astype(jnp.int32)
    dt0 = jnp.exp(jax.random.uniform(next(ks), (DEPTH, S_HEADS), jnp.float32,
                                     math.log(1e-3), math.log(1e-1)))
    return {
        'x_prompt': nrm((BATCH, SEQ, D_MODEL)),
        'x_sample': nrm((DEC_BATCH, DEC_SEQ, D_MODEL)),
        'cache_diff_k': nrm((DEPTH, n_phys, PAGE_SIZE, A_HEADS, 2, HEAD_DIM)),
        'cache_diff_v': nrm((DEPTH, n_phys, PAGE_SIZE, A_HEADS, 2 * HEAD_DIM)),
        'cache_fox_k': nrm((DEPTH, n_phys, PAGE_SIZE, F_HEADS, HEAD_DIM)),
        'cache_fox_v': nrm((DEPTH, n_phys, PAGE_SIZE, F_HEADS, HEAD_DIM)),
        'cache_fox_logf': jax.nn.log_sigmoid(3.0 + nrm((DEPTH, n_phys, PAGE_SIZE, F_HEADS))),
        'state_ssm': nrm((DEPTH, DEC_BATCH, S_HEADS, S_HEADDIM, D_STATE), 0.5),
        'state_conv': nrm((DEPTH, DEC_BATCH, CONV_W - 1, CONV_DIM)),
        'page_table': page_table,
        'c_prompt': nrm((BATCH, D_MODEL)),
        'c_sample': nrm((DEC_BATCH, D_MODEL)),
        'w_ada': nrm((DEPTH, D_MODEL, 6 * D_MODEL), 0.5 * D_MODEL ** -0.5),
        'b_ada': nrm((DEPTH, 6 * D_MODEL), 0.02),
        'norm_mix': 1.0 + nrm((DEPTH, D_MODEL), 0.02),
        'norm_ffn': 1.0 + nrm((DEPTH, D_MODEL), 0.02),
        'w_in': nrm((DEPTH, D_MODEL, IN_COLS), D_MODEL ** -0.5),
        'diff_lambda': nrm((DEPTH, 4, HEAD_DIM), 0.1),
        'diff_subln': 1.0 + nrm((DEPTH, 2 * HEAD_DIM), 0.02),
        'fox_b_f': 3.0 + nrm((DEPTH, F_HEADS), 0.5),
        'ssm_conv_w': nrm((DEPTH, CONV_W, CONV_DIM), CONV_W ** -0.5),
        'ssm_conv_b': nrm((DEPTH, CONV_DIM), 0.02),
        'ssm_dt_bias': dt0 + jnp.log(-jnp.expm1(-dt0)),
        'ssm_a_log': jnp.log(jax.random.uniform(next(ks), (DEPTH, S_HEADS), jnp.float32, 1.0, 16.0)),
        'ssm_d': 1.0 + nrm((DEPTH, S_HEADS), 0.02),
        'ssm_norm': 1.0 + nrm((DEPTH, S_WIDTH), 0.02),
        'w_branch_a': nrm((DEPTH, A_WIDTH, D_MODEL), A_WIDTH ** -0.5),
        'w_branch_f': nrm((DEPTH, F_WIDTH, D_MODEL), F_WIDTH ** -0.5),
        'w_branch_s': nrm((DEPTH, S_WIDTH, D_MODEL), S_WIDTH ** -0.5),
        'w_out': nrm((DEPTH, D_MODEL, D_MODEL), D_MODEL ** -0.5),
        'router_w': nrm((DEPTH, D_MODEL, N_EXPERTS), D_MODEL ** -0.5),
        'router_bias': nrm((DEPTH, N_EXPERTS), 0.01),
        'exp_w_gate': nrm((DEPTH, N_EXPERTS, D_MODEL, D_EXPERT), D_MODEL ** -0.5),
        'exp_w_up': nrm((DEPTH, N_EXPERTS, D_MODEL, D_EXPERT), D_MODEL ** -0.5),
        'exp_w_down': nrm((DEPTH, N_EXPERTS, D_EXPERT, D_MODEL), D_EXPERT ** -0.5),
        'shared_w_gate': nrm((DEPTH, D_MODEL, D_SHARED), D_MODEL ** -0.5),
        'shared_w_up': nrm((DEPTH, D_MODEL, D_SHARED), D_MODEL ** -0.5),
        'shared_w_down': nrm((DEPTH, D_SHARED, D_MODEL), D_SHARED ** -0.5),
        'norm_final': 1.0 + nrm((D_MODEL,), 0.02),
    }


def reference(x_prompt, x_sample, cache_diff_k, cache_diff_v, cache_fox_k, cache_fox_v, cache_fox_logf,
              state_ssm, state_conv, page_table, c_prompt, c_sample, w_ada, b_ada, norm_mix, norm_ffn,
              w_in, diff_lambda, diff_subln, fox_b_f, ssm_conv_w, ssm_conv_b, ssm_dt_bias, ssm_a_log,
              ssm_d, ssm_norm, w_branch_a, w_branch_f, w_branch_s, w_out, router_w, router_bias,
              exp_w_gate, exp_w_up, exp_w_down, shared_w_gate, shared_w_up, shared_w_down, norm_final):
    xp, xs = x_prompt, x_sample
    st_p, st_s = [], []
    for l in range(DEPTH):
        p = {'w_in': w_in[l], 'diff_lambda': diff_lambda[l], 'diff_subln': diff_subln[l],
             'fox_b_f': fox_b_f[l], 'ssm_conv_w': ssm_conv_w[l], 'ssm_conv_b': ssm_conv_b[l],
             'ssm_dt_bias': ssm_dt_bias[l], 'ssm_a_log': ssm_a_log[l], 'ssm_d': ssm_d[l],
             'ssm_norm': ssm_norm[l], 'w_branch_a': w_branch_a[l], 'w_branch_f': w_branch_f[l],
             'w_branch_s': w_branch_s[l], 'w_out': w_out[l], 'router_w': router_w[l],
             'router_bias': router_bias[l], 'exp_w_gate': exp_w_gate[l], 'exp_w_up': exp_w_up[l],
             'exp_w_down': exp_w_down[l], 'shared_w_gate': shared_w_gate[l],
             'shared_w_up': shared_w_up[l], 'shared_w_down': shared_w_down[l]}
        xp, s_p = residual_layer(xp, adaln(c_prompt, w_ada[l], b_ada[l]), norm_mix[l], norm_ffn[l],
                                 lambda h: mixer_prompt(h, p, l), p)
        xs, s_s = residual_layer(
            xs, adaln(c_sample, w_ada[l], b_ada[l]), norm_mix[l], norm_ffn[l],
            lambda h: mixer_sample(h, p, l, page_table, cache_diff_k, cache_diff_v, cache_fox_k,
                                   cache_fox_v, cache_fox_logf, state_ssm[l], state_conv[l]), p)
        st_p.append(s_p)
        st_s.append(s_s)
    y_prompt = rmsnorm(xp, norm_final)
    y_sample = rmsnorm(xs, norm_final)
    stk = lambda states, i: jnp.stack([s[i] for s in states])
    return (y_prompt, y_sample,
            stk(st_p, 0), stk(st_s, 0), stk(st_p, 1), stk(st_s, 1),
            stk(st_p, 2), stk(st_s, 2), stk(st_p, 3), stk(st_s, 3),
            stk(st_p, 4), stk(st_s, 4), stk(st_p, 5), stk(st_s, 5),
            stk(st_p, 6), stk(st_s, 6))
```

```python
import functools
import math

import jax
import jax.numpy as jnp
from jax import lax
from jax.experimental import pallas as pl
from jax.experimental.pallas import tpu as pltpu

F32 = jnp.float32
BF16 = jnp.bfloat16
I32 = jnp.int32

HEAD_DIM = 64
SCALE = HEAD_DIM ** -0.5
A_HEADS = 4
F_HEADS = 8
S_HEADS = 8
S_HEADDIM = 64
S_GROUPS = 2
HPG = S_HEADS // S_GROUPS
D_STATE = 64
CONV_W = 4
ROT_DIM = HEAD_DIM // 4
ROPE_THETA = 500000.0
A_WIDTH = A_HEADS * 2 * HEAD_DIM
F_WIDTH = F_HEADS * HEAD_DIM
S_WIDTH = S_HEADS * S_HEADDIM
CONV_DIM = S_WIDTH + 2 * S_GROUPS * D_STATE
N_BRANCH = 3
N_EXPERTS = 64
N_EXPERT_GROUPS = 8
TOPK_GROUPS = 4
TOP_K = 8
ROUTED_SCALE = 2.5
EPS = 1e-6
NEG = -1e30
LANES = 128
SMALL_W = LANES
VMEM_LIMIT = 56 * 1024 * 1024

_OFF = {}
_o = 0
for _n, _w in (("aq", A_WIDTH), ("ak", A_WIDTH), ("av", A_WIDTH), ("fq", F_WIDTH), ("fk", F_WIDTH),
               ("fv", F_WIDTH), ("sz", S_WIDTH), ("sxbc", CONV_DIM), ("small", SMALL_W)):
    _OFF[_n] = (_o, _o + _w)
    _o += _w
PACK_COLS = _o


def _cparams(sem):
    return pltpu.CompilerParams(dimension_semantics=sem, vmem_limit_bytes=VMEM_LIMIT)


def _dot(a, b):
    return jnp.dot(a, b, preferred_element_type=F32)


def _dot_nt(a, b):
    return lax.dot_general(a, b, (((1,), (1,)), ((), ())), preferred_element_type=F32)


def _silu(x):
    return x * (1.0 / (1.0 + jnp.exp(-x)))


def _sigmoid(x):
    return 1.0 / (1.0 + jnp.exp(-x))


def _norm_mod(x, nw, scale, shift):
    ms = jnp.mean(x * x, axis=-1, keepdims=True)
    y = x * lax.rsqrt(ms + EPS) * nw
    return y * (1.0 + scale) + shift


def _ada_kernel(c_ref, w_ref, b_ref, o_ref):
    c = _silu(c_ref[...]).astype(BF16)
    o_ref[...] = _dot(c, w_ref[...].astype(BF16)) + b_ref[...]


def _ada(c_all, w, b, tn=512):
    m, d = c_all.shape
    n = w.shape[1]
    return pl.pallas_call(
        _ada_kernel,
        grid=(n // tn,),
        in_specs=[pl.BlockSpec((m, d), lambda j: (0, 0)),
                  pl.BlockSpec((d, tn), lambda j: (0, j)),
                  pl.BlockSpec((1, tn), lambda j: (0, j))],
        out_specs=pl.BlockSpec((m, tn), lambda j: (0, j)),
        out_shape=jax.ShapeDtypeStruct((m, n), F32),
        compiler_params=_cparams(("arbitrary",)),
        name="ada",
    )(c_all, w, b.reshape(1, n))


def _proj_kernel(x_ref, sh_ref, sc_ref, nw_ref, w_ref, bias_ref, rc_ref, rn_ref, rp_ref,
                 qa_ref, ka_ref, kab_ref, va_ref, vab_ref,
                 qf_ref, kf_ref, kfb_ref, vf_ref, vfb_ref,
                 sz_ref, sxbc_ref, small_ref, lf_ref, cum_ref, run_ref, *, tiles_per_batch):
    i = pl.program_id(0)
    h = _norm_mod(x_ref[...], nw_ref[...], sc_ref[...], sh_ref[...]).astype(BF16)
    tm = h.shape[0]

    def col(name):
        a, b = _OFF[name]
        return _dot(h, w_ref[:, a:b])

    rc = jnp.concatenate([rc_ref[...]] * (A_WIDTH // LANES), axis=1)
    rn = jnp.concatenate([rn_ref[...]] * (A_WIDTH // LANES), axis=1)
    rp = jnp.concatenate([rp_ref[...]] * (A_WIDTH // LANES), axis=1)

    def rope(v):
        half = ROT_DIM // 2
        parts = []
        for j in range(A_WIDTH // LANES):
            vj = v[:, j * LANES:(j + 1) * LANES]
            parts.append((pltpu.roll(vj, LANES - half, axis=1), pltpu.roll(vj, half, axis=1)))
        left = jnp.concatenate([p[0] for p in parts], axis=1)
        right = jnp.concatenate([p[1] for p in parts], axis=1)
        return v * rc + left * rn + right * rp

    aq = rope(col("aq"))
    qa_ref[...] = (aq * SCALE).astype(BF16)
    ak = rope(col("ak"))
    ka_ref[...] = ak
    kab_ref[...] = ak.astype(BF16)
    av = col("av")
    va_ref[...] = av
    vab_ref[...] = av.astype(BF16)
    qf_ref[...] = (col("fq") * SCALE).astype(BF16)
    fk = col("fk")
    kf_ref[...] = fk
    kfb_ref[...] = fk.astype(BF16)
    fv = col("fv")
    vf_ref[...] = fv
    vfb_ref[...] = fv.astype(BF16)
    sz_ref[...] = col("sz")
    sxbc_ref[...] = col("sxbc")
    small = col("small")
    small_ref[...] = small
    z = small + bias_ref[...]
    lf = jnp.minimum(z, 0.0) - jnp.log(1.0 + jnp.exp(-jnp.abs(z)))
    lf_ref[...] = lf

    @pl.when(i % tiles_per_batch == 0)
    def _():
        run_ref[...] = jnp.zeros_like(run_ref)

    r = lax.broadcasted_iota(I32, (tm, tm), 0)
    c = lax.broadcasted_iota(I32, (tm, tm), 1)
    tri = jnp.where(c <= r, 1.0, 0.0).astype(F32)
    cum = jnp.dot(tri, lf, preferred_element_type=F32, precision=lax.Precision.HIGHEST) + run_ref[...]
    cum_ref[...] = cum
    run_ref[...] = cum[tm - 1:tm, :]


def _proj(x, shift, scale, nw, w_pack, bias_small, rope_tabs, *, tm, tiles_per_batch, per_row_mod):
    n, d = x.shape
    nt = n // tm
    if per_row_mod:
        mod_spec = pl.BlockSpec((tm, d), lambda i: (i, 0))
    else:
        mod_spec = pl.BlockSpec((None, 1, d), lambda i: (i // tiles_per_batch, 0, 0))
    tab_spec = pl.BlockSpec((tm, LANES), lambda i: (i % tiles_per_batch, 0))
    row = lambda w: pl.BlockSpec((tm, w), lambda i: (i, 0))
    f = lambda w, dt: jax.ShapeDtypeStruct((n, w), dt)
    out_shapes = [f(A_WIDTH, BF16), f(A_WIDTH, F32), f(A_WIDTH, BF16), f(A_WIDTH, F32), f(A_WIDTH, BF16),
                  f(F_WIDTH, BF16), f(F_WIDTH, F32), f(F_WIDTH, BF16), f(F_WIDTH, F32), f(F_WIDTH, BF16),
                  f(S_WIDTH, F32), f(CONV_DIM, F32), f(SMALL_W, F32), f(SMALL_W, F32), f(SMALL_W, F32)]
    out_specs = [row(s.shape[1]) for s in out_shapes]
    return pl.pallas_call(
        functools.partial(_proj_kernel, tiles_per_batch=tiles_per_batch),
        grid=(nt,),
        in_specs=[row(d), mod_spec, mod_spec, pl.BlockSpec((1, d), lambda i: (0, 0)),
                  pl.BlockSpec((d, PACK_COLS), lambda i: (0, 0)),
                  pl.BlockSpec((1, SMALL_W), lambda i: (0, 0)),
                  tab_spec, tab_spec, tab_spec],
        out_specs=out_specs,
        out_shape=out_shapes,
        scratch_shapes=[pltpu.VMEM((1, SMALL_W), F32)],
        compiler_params=_cparams(("arbitrary",)),
        name="proj",
    )(x, shift, scale, nw, w_pack, bias_small, *rope_tabs)


def _lam(lp_ref):
    lp = lp_ref[...]
    s01 = jnp.sum(lp[0:1, :] * lp[1:2, :], axis=-1, keepdims=True)
    s23 = jnp.sum(lp[2:3, :] * lp[3:4, :], axis=-1, keepdims=True)
    return jnp.exp(s01) - jnp.exp(s23)


def _flash_loop(q2, k_ref, v_ref, m_ref, l_ref, acc_ref, qi, tk, bias_fn):
    rows = q2.shape[0]
    tq = rows // 2
    m_ref[...] = jnp.full(m_ref.shape, -jnp.inf, F32)
    l_ref[...] = jnp.zeros(l_ref.shape, F32)
    acc_ref[...] = jnp.zeros(acc_ref.shape, F32)

    def step(kj, masked):
        k0 = pl.multiple_of(kj * tk, tk)
        kb = k_ref[pl.ds(k0, tk), :]
        vb = v_ref[pl.ds(k0, tk), :]
        s = _dot_nt(q2, kb)
        if bias_fn is not None:
            s = s + bias_fn(kj)
        if masked:
            r = lax.broadcasted_iota(I32, (rows, tk), 0)
            c = lax.broadcasted_iota(I32, (rows, tk), 1)
            r = jnp.where(r >= tq, r - tq, r)
            s = jnp.where(c <= r, s, NEG)
        m_old = m_ref[...]
        m_new = jnp.maximum(m_old, jnp.max(s, axis=-1, keepdims=True))
        alpha = jnp.exp(m_old - m_new)
        p = jnp.exp(s - m_new)
        l_ref[...] = alpha * l_ref[...] + jnp.sum(p, axis=-1, keepdims=True)
        acc_ref[...] = alpha * acc_ref[...] + _dot(p.astype(BF16), vb)
        m_ref[...] = m_new

    def body(kj, carry):
        step(kj, False)
        return carry

    lax.fori_loop(0, qi, body, 0)
    step(qi, True)


def _diff_attn_kernel(q_ref, k_ref, v_ref, lp_ref, sub_ref, o_ref, m_ref, l_ref, acc_ref, *, lam_init, tk):
    qi = pl.program_id(2)
    q = q_ref[...]
    tq = q.shape[0]
    lane = lax.broadcasted_iota(I32, (tq, LANES), 1)
    zero = jnp.zeros_like(q)
    q2 = jnp.concatenate([jnp.where(lane < HEAD_DIM, q, zero), jnp.where(lane >= HEAD_DIM, q, zero)], axis=0)
    _flash_loop(q2, k_ref, v_ref, m_ref, l_ref, acc_ref, qi, tk, None)
    lam = _lam(lp_ref) + lam_init
    o = acc_ref[...] / l_ref[...]
    o = o[:tq] - lam * o[tq:]
    ms = jnp.mean(o * o, axis=-1, keepdims=True)
    o = o * lax.rsqrt(ms + EPS) * sub_ref[...] * (1.0 - lam_init)
    o_ref[...] = o.astype(o_ref.dtype)


def _diff_attn(q, k, v, lam_params, subln, lam_init, *, nb, seq, tq):
    n = q.shape[0]
    nq = seq // tq
    qspec = pl.BlockSpec((tq, LANES), lambda b, h, i: (b * nq + i, h))
    kvspec = pl.BlockSpec((seq, LANES), lambda b, h, i: (b, h))
    return pl.pallas_call(
        functools.partial(_diff_attn_kernel, lam_init=lam_init, tk=tq),
        grid=(nb, A_HEADS, nq),
        in_specs=[qspec, kvspec, kvspec,
                  pl.BlockSpec((4, HEAD_DIM), lambda b, h, i: (0, 0)),
                  pl.BlockSpec((1, LANES), lambda b, h, i: (0, 0))],
        out_specs=qspec,
        out_shape=jax.ShapeDtypeStruct((n, A_WIDTH), BF16),
        scratch_shapes=[pltpu.VMEM((2 * tq, 1), F32), pltpu.VMEM((2 * tq, 1), F32),
                        pltpu.VMEM((2 * tq, LANES), F32)],
        compiler_params=_cparams(("arbitrary", "arbitrary", "arbitrary")),
        name="diff_attn",
    )(q, k, v, lam_params, subln.reshape(1, LANES))


def _fox_attn_kernel(q_ref, k_ref, v_ref, cq_ref, ck_ref, o_ref, m_ref, l_ref, acc_ref, *, tk):
    hp = pl.program_id(1)
    qi = pl.program_id(2)
    q = q_ref[...]
    tq = q.shape[0]
    lane = lax.broadcasted_iota(I32, (tq, LANES), 1)
    zero = jnp.zeros_like(q)
    q2 = jnp.concatenate([jnp.where(lane < HEAD_DIM, q, zero), jnp.where(lane >= HEAD_DIM, q, zero)], axis=0)
    cq = cq_ref[...]
    cq0 = jnp.sum(jnp.where(lane == 2 * hp, cq, 0.0), axis=-1, keepdims=True)
    cq1 = jnp.sum(jnp.where(lane == 2 * hp + 1, cq, 0.0), axis=-1, keepdims=True)
    cqc = jnp.concatenate([cq0, cq1], axis=0)
    rowsel = lax.broadcasted_iota(I32, (2 * tq, 1), 0) < tq

    def bias(kj):
        ck = ck_ref[kj]
        sub = lax.broadcasted_iota(I32, ck.shape, 0)
        ck0 = jnp.sum(jnp.where(sub == 2 * hp, ck, 0.0), axis=0, keepdims=True)
        ck1 = jnp.sum(jnp.where(sub == 2 * hp + 1, ck, 0.0), axis=0, keepdims=True)
        return cqc - jnp.where(rowsel, ck0, ck1)

    _flash_loop(q2, k_ref, v_ref, m_ref, l_ref, acc_ref, qi, tk, bias)
    o = acc_ref[...] / l_ref[...]
    o_ref[...] = jnp.where(lane < HEAD_DIM, o[:tq], o[tq:]).astype(o_ref.dtype)


def _fox_attn(q, k, v, cum, cum_t, *, nb, seq, tq):
    n = q.shape[0]
    nq = seq // tq
    qspec = pl.BlockSpec((tq, LANES), lambda b, h, i: (b * nq + i, h))
    kvspec = pl.BlockSpec((seq, LANES), lambda b, h, i: (b, h))
    return pl.pallas_call(
        functools.partial(_fox_attn_kernel, tk=tq),
        grid=(nb, F_HEADS // 2, nq),
        in_specs=[qspec, kvspec, kvspec,
                  pl.BlockSpec((tq, LANES), lambda b, h, i: (b * nq + i, 0)),
                  pl.BlockSpec((None, nq, F_HEADS, tq), lambda b, h, i: (b, 0, 0, 0))],
        out_specs=qspec,
        out_shape=jax.ShapeDtypeStruct((n, F_WIDTH), BF16),
        scratch_shapes=[pltpu.VMEM((2 * tq, 1), F32), pltpu.VMEM((2 * tq, 1), F32),
                        pltpu.VMEM((2 * tq, LANES), F32)],
        compiler_params=_cparams(("arbitrary", "arbitrary", "arbitrary")),
        name="fox_attn",
    )(q, k, v, cum, cum_t)


def _ssd_kernel(xbc_ref, z_ref, small_ref, cw_ref, cb_ref, dtb_ref, alog_ref, dsk_ref, nw_ref,
                o_ref, hout_ref, cs_ref, ext_ref, h_ref, *, chunk):
    c = pl.program_id(1)
    nc = pl.num_programs(1)
    L = chunk
    pad = 8

    @pl.when(c == 0)
    def _():
        ext_ref[0:pad, :] = jnp.zeros((pad, CONV_DIM), F32)
        h_ref[...] = jnp.zeros_like(h_ref)

    xin = xbc_ref[...]
    ext_ref[pad:pad + L, :] = xin
    y = cb_ref[...] + cw_ref[CONV_W - 1:CONV_W, :] * xin
    for j in range(CONV_W - 1):
        sft = CONV_W - 1 - j
        y = y + cw_ref[j:j + 1, :] * ext_ref[pad - sft:pad - sft + L, :]
    tail = xin[L - pad:L, :]
    ext_ref[0:pad, :] = tail
    cs_ref[...] = tail
    xbc = _silu(y)
    xs = xbc[:, :S_WIDTH]

    dt_raw = small_ref[...] + dtb_ref[...]
    dt = jnp.maximum(dt_raw, 0.0) + jnp.log(1.0 + jnp.exp(-jnp.abs(dt_raw)))
    da = dt * (-jnp.exp(alog_ref[...]))
    r = lax.broadcasted_iota(I32, (L, L), 0)
    cc = lax.broadcasted_iota(I32, (L, L), 1)
    causal = cc <= r
    tri = jnp.where(causal, 1.0, 0.0).astype(F32)
    a_cs = jnp.dot(tri, da, preferred_element_type=F32, precision=lax.Precision.HIGHEST)
    a_cs_t = a_cs.T
    a_end = a_cs[L - 1:L, :]
    to_end = jnp.exp(a_end - a_cs)
    exp_cs = jnp.exp(a_cs)
    exp_end = jnp.exp(a_end)

    xds = [xs[:, hd * S_HEADDIM:(hd + 1) * S_HEADDIM] * dt[:, 8 + hd:9 + hd] for hd in range(S_HEADS)]
    xw_t = jnp.concatenate([xds[hd] * to_end[:, 8 + hd:9 + hd] for hd in range(S_HEADS)], axis=1).T
    ys = []
    for g in range(S_GROUPS):
        bm = xbc[:, S_WIDTH + g * D_STATE:S_WIDTH + (g + 1) * D_STATE].astype(BF16)
        cm = xbc[:, S_WIDTH + (S_GROUPS + g) * D_STATE:S_WIDTH + (S_GROUPS + g + 1) * D_STATE].astype(BF16)
        cb = _dot_nt(cm, bm)
        for rr in range(HPG):
            hd = g * HPG + rr
            col = 8 + hd
            x_h = xs[:, hd * S_HEADDIM:(hd + 1) * S_HEADDIM]
            xd = xds[hd]
            seg = a_cs[:, col:col + 1] - a_cs_t[col:col + 1, :]
            decay = jnp.where(causal, jnp.exp(jnp.where(causal, seg, 0.0)), 0.0)
            y_diag = _dot((cb * decay).astype(BF16), xd.astype(BF16))
            h_prev = h_ref[hd]
            y_off = _dot_nt(cm, h_prev.astype(BF16)) * exp_cs[:, col:col + 1]
            states = _dot(xw_t[hd * S_HEADDIM:(hd + 1) * S_HEADDIM, :].astype(BF16), bm)
            h_ref[hd] = h_prev * exp_end[:, col:col + 1] + states
            ys.append(y_diag + y_off + x_h * dsk_ref[:, col:col + 1])
    yy = jnp.concatenate(ys, axis=1) * _silu(z_ref[...])
    gw = S_WIDTH // S_GROUPS
    outs = []
    for g in range(S_GROUPS):
        yg = yy[:, g * gw:(g + 1) * gw]
        ms = jnp.mean(yg * yg, axis=-1, keepdims=True)
        outs.append(yg * lax.rsqrt(ms + EPS) * nw_ref[:, g * gw:(g + 1) * gw])
    o_ref[...] = jnp.concatenate(outs, axis=1).astype(o_ref.dtype)

    @pl.when(c == nc - 1)
    def _():
        hout_ref[...] = h_ref[...]


def _lane_pack(v, off=8):
    return jnp.zeros((1, SMALL_W), F32).at[0, off:off + v.shape[0]].set(v.astype(F32))


def _ssd_prompt(sxbc, sz, small, p, *, nb, seq, chunk):
    n = sxbc.shape[0]
    ncn = seq // chunk
    row = lambda w: pl.BlockSpec((chunk, w), lambda b, c: (b * ncn + c, 0))
    full = lambda a: pl.BlockSpec(a.shape, lambda b, c: (0,) * a.ndim)
    cw = p['ssm_conv_w']
    cb = p['ssm_conv_b'].reshape(1, CONV_DIM)
    dtb = _lane_pack(p['ssm_dt_bias'])
    alog = _lane_pack(p['ssm_a_log'])
    dsk = _lane_pack(p['ssm_d'])
    nw = p['ssm_norm'].reshape(1, S_WIDTH)
    o, hout, cs = pl.pallas_call(
        functools.partial(_ssd_kernel, chunk=chunk),
        grid=(nb, ncn),
        in_specs=[row(CONV_DIM), row(S_WIDTH), row(SMALL_W), full(cw), full(cb), full(dtb), full(alog),
                  full(dsk), full(nw)],
        out_specs=[row(S_WIDTH),
                   pl.BlockSpec((None, S_HEADS, S_HEADDIM, D_STATE), lambda b, c: (b, 0, 0, 0)),
                   pl.BlockSpec((None, 8, CONV_DIM), lambda b, c: (b, 0, 0))],
        out_shape=[jax.ShapeDtypeStruct((n, S_WIDTH), BF16),
                   jax.ShapeDtypeStruct((nb, S_HEADS, S_HEADDIM, D_STATE), F32),
                   jax.ShapeDtypeStruct((nb, 8, CONV_DIM), F32)],
        scratch_shapes=[pltpu.VMEM((8 + chunk, CONV_DIM), F32),
                        pltpu.VMEM((S_HEADS, S_HEADDIM, D_STATE), F32)],
        compiler_params=_cparams(("arbitrary", "arbitrary")),
        name="ssd",
    )(sxbc, sz, small, cw, cb, dtb, alog, dsk, nw)
    return o, hout, cs[:, 8 - (CONV_W - 1):, :]


def _ssd_step_kernel(xbc_ref, buf_ref, z_ref, small_ref, h_ref, cw_ref, cb_ref, dtb_ref, alog_ref, dsk_ref,
                     nw_ref, o_ref, hout_ref, cs_ref):
    xin = xbc_ref[...]
    buf = buf_ref[...]
    y = cb_ref[...] + cw_ref[CONV_W - 1:CONV_W, :] * xin
    for j in range(CONV_W - 1):
        y = y + cw_ref[j:j + 1, :] * buf[j:j + 1, :]
    cs_ref[...] = jnp.concatenate([buf[1:CONV_W - 1, :], xin], axis=0)
    xbc = _silu(y)
    xs = xbc[:, :S_WIDTH]
    dt_raw = small_ref[...] + dtb_ref[...]
    dt = jnp.maximum(dt_raw, 0.0) + jnp.log(1.0 + jnp.exp(-jnp.abs(dt_raw)))
    dec = jnp.exp(dt * (-jnp.exp(alog_ref[...])))
    rows = S_HEADS * S_HEADDIM
    er = lax.broadcasted_iota(I32, (SMALL_W, rows), 0)
    ec = lax.broadcasted_iota(I32, (SMALL_W, rows), 1)
    expand = jnp.where(er - 8 == ec // S_HEADDIM, 1.0, 0.0).astype(F32)
    stacked = jnp.concatenate([dt, dec, dsk_ref[...], jnp.zeros((5, SMALL_W), F32)], axis=0)
    per_pos = jnp.dot(stacked, expand, preferred_element_type=F32, precision=lax.Precision.HIGHEST)
    dt_row, dec_row, dsk_row = per_pos[0:1, :], per_pos[1:2, :], per_pos[2:3, :]
    xd_row = xs * dt_row

    def to_col(v):
        return jnp.broadcast_to(v, (LANES, rows)).T

    xd_col = to_col(xd_row)[:, :D_STATE]
    dec_col = to_col(dec_row)[:, :D_STATE]
    half = rows // S_GROUPS
    bsel = lax.broadcasted_iota(I32, (rows, D_STATE), 0) < half
    b0 = xbc[:, S_WIDTH:S_WIDTH + D_STATE]
    b1 = xbc[:, S_WIDTH + D_STATE:S_WIDTH + 2 * D_STATE]
    c0 = xbc[:, S_WIDTH + 2 * D_STATE:S_WIDTH + 3 * D_STATE]
    c1 = xbc[:, S_WIDTH + 3 * D_STATE:S_WIDTH + 4 * D_STATE]
    bmat = jnp.where(bsel, jnp.broadcast_to(b0, (rows, D_STATE)), jnp.broadcast_to(b1, (rows, D_STATE)))
    cmat = jnp.where(bsel, jnp.broadcast_to(c0, (rows, D_STATE)), jnp.broadcast_to(c1, (rows, D_STATE)))
    h = h_ref[...].reshape(rows, D_STATE)
    hn = h * dec_col + xd_col * bmat
    hout_ref[...] = hn.reshape(S_HEADS, S_HEADDIM, D_STATE)
    ycol = jnp.sum(hn.astype(BF16).astype(F32) * cmat.astype(BF16).astype(F32), axis=-1, keepdims=True)
    yrow = jnp.broadcast_to(ycol, (rows, LANES)).T[0:1, :]
    yy = (yrow + xs * dsk_row) * _silu(z_ref[...])
    gw = S_WIDTH // S_GROUPS
    outs = []
    for g in range(S_GROUPS):
        yg = yy[:, g * gw:(g + 1) * gw]
        ms = jnp.mean(yg * yg, axis=-1, keepdims=True)
        outs.append(yg * lax.rsqrt(ms + EPS) * nw_ref[:, g * gw:(g + 1) * gw])
    o_ref[...] = jnp.concatenate(outs, axis=1).astype(o_ref.dtype)


def _ssd_sample(sxbc, conv_buf, sz, small, h0, p):
    nb = sxbc.shape[0]
    cw = p['ssm_conv_w']
    cb = p['ssm_conv_b'].reshape(1, CONV_DIM)
    dtb = _lane_pack(p['ssm_dt_bias'])
    alog = _lane_pack(p['ssm_a_log'])
    dsk = _lane_pack(p['ssm_d'])
    nw = p['ssm_norm'].reshape(1, S_WIDTH)
    full = lambda a: pl.BlockSpec(a.shape, lambda b: (0,) * a.ndim)
    r3 = lambda w: pl.BlockSpec((None, 1, w), lambda b: (b, 0, 0))
    o, hout, cs = pl.pallas_call(
        _ssd_step_kernel,
        grid=(nb,),
        in_specs=[r3(CONV_DIM), pl.BlockSpec((None, CONV_W - 1, CONV_DIM), lambda b: (b, 0, 0)),
                  r3(S_WIDTH), r3(SMALL_W),
                  pl.BlockSpec((None, S_HEADS, S_HEADDIM, D_STATE), lambda b: (b, 0, 0, 0)),
                  full(cw), full(cb), full(dtb), full(alog), full(dsk), full(nw)],
        out_specs=[r3(S_WIDTH),
                   pl.BlockSpec((None, S_HEADS, S_HEADDIM, D_STATE), lambda b: (b, 0, 0, 0)),
                   pl.BlockSpec((None, CONV_W - 1, CONV_DIM), lambda b: (b, 0, 0))],
        out_shape=[jax.ShapeDtypeStruct((nb, 1, S_WIDTH), BF16),
                   jax.ShapeDtypeStruct((nb, S_HEADS, S_HEADDIM, D_STATE), F32),
                   jax.ShapeDtypeStruct((nb, CONV_W - 1, CONV_DIM), F32)],
        compiler_params=_cparams(("arbitrary",)),
        name="ssd_step",
    )(sxbc.reshape(nb, 1, CONV_DIM), conv_buf, sz.reshape(nb, 1, S_WIDTH), small.reshape(nb, 1, SMALL_W),
      h0, cw, cb, dtb, alog, dsk, nw)
    return o.reshape(nb, S_WIDTH), hout, cs


def _decode_kernel(pt_ref, qa_ref, kan_ref, van_ref, qf_ref, kfn_ref, vfn_ref, lfn_ref, lp_ref, sub_ref,
                   *refs, lam_init, pages_per_step, n_steps):
    P = pages_per_step
    kd_refs = refs[0:P]
    kf_refs = refs[P:2 * P]
    lf_refs = refs[2 * P:3 * P]
    vd_refs = refs[3 * P:4 * P]
    vf_refs = refs[4 * P:5 * P]
    oa_ref, of_ref = refs[5 * P:5 * P + 2]
    (qd_ref, qfm_ref, sd_ref, sf_ref, md_ref, mf_ref, ld_ref, lfs_ref, accd_ref, accf_ref, suf_ref,
     ssd_ref, ssf_ref) = refs[5 * P + 2:]
    j = pl.program_id(1)
    W = A_WIDTH
    n_pages = n_steps * P
    bf = lambda a: a.astype(BF16).astype(F32)

    @pl.when(j == 0)
    def _():
        seg16 = lax.broadcasted_iota(I32, (16, W), 1) // HEAD_DIM
        row16 = lax.broadcasted_iota(I32, (16, W), 0)
        want = jnp.where(row16 < 8, 2 * row16, 2 * (row16 - 8) + 1)
        keep = jnp.logical_and(seg16 == want, (row16 % 8) < A_HEADS)
        qd_ref[...] = jnp.where(keep, jnp.broadcast_to(qa_ref[...].astype(F32), (16, W)), 0.0).astype(BF16)
        seg8 = lax.broadcasted_iota(I32, (8, W), 1) // HEAD_DIM
        row8 = lax.broadcasted_iota(I32, (8, W), 0)
        qfm_ref[...] = jnp.where(seg8 == row8, jnp.broadcast_to(qf_ref[...].astype(F32), (8, W)), 0.0).astype(BF16)
        md_ref[...] = jnp.full(md_ref.shape, -jnp.inf, F32)
        mf_ref[...] = jnp.full(mf_ref.shape, -jnp.inf, F32)
        accd_ref[...] = jnp.zeros_like(accd_ref)
        accf_ref[...] = jnp.zeros_like(accf_ref)
        suf_ref[...] = jnp.zeros_like(suf_ref)

    lane = lax.broadcasted_iota(I32, (8, PAGE), 1)

    @pl.when(j < n_steps)
    def _():
        for pp in range(P):
            g = j * P + pp
            sd = _dot_nt(qd_ref[...], kd_refs[pp][...].astype(BF16))
            sd_ref[g] = sd
            md_ref[...] = jnp.maximum(md_ref[...], jnp.max(sd, axis=-1, keepdims=True))
            lf = lf_refs[pp][...]
            inc = lf
            d = 1
            while d < PAGE:
                inc = inc + jnp.where(lane + d < PAGE, pltpu.roll(inc, PAGE - d, axis=1), 0.0)
                d *= 2
            sfx = inc - lf + suf_ref[...]
            sf = (_dot_nt(qfm_ref[...], kf_refs[pp][...].astype(BF16)) + lfn_ref[...]) + sfx
            sf_ref[g] = sf
            mf_ref[...] = jnp.maximum(mf_ref[...], jnp.max(sf, axis=-1, keepdims=True))
            suf_ref[...] = suf_ref[...] + jnp.sum(lf, axis=-1, keepdims=True)

    @pl.when(j == n_steps)
    def _():
        s_self_d = jnp.sum(qd_ref[...].astype(F32) * bf(kan_ref[...]), axis=-1, keepdims=True)
        dnew = lfn_ref[...]
        s_self_f = (jnp.sum(qfm_ref[...].astype(F32) * bf(kfn_ref[...]), axis=-1, keepdims=True) + dnew) - dnew
        m_d = jnp.maximum(md_ref[...], s_self_d)
        m_f = jnp.maximum(mf_ref[...], s_self_f)
        e_self_d = jnp.exp(s_self_d - m_d)
        e_self_f = jnp.exp(s_self_f - m_f)

        def body(g, carry):
            l_d, l_f = carry
            e_d = jnp.exp(sd_ref[g] - m_d)
            sd_ref[g] = e_d
            e_f = jnp.exp(sf_ref[g] - m_f)
            sf_ref[g] = e_f
            return (l_d + jnp.sum(e_d, axis=-1, keepdims=True), l_f + jnp.sum(e_f, axis=-1, keepdims=True))

        l_d, l_f = lax.fori_loop(0, n_pages, body, (jnp.zeros((16, 1), F32), jnp.zeros((8, 1), F32)))
        ld_ref[...] = l_d + e_self_d
        lfs_ref[...] = l_f + e_self_f
        ssd_ref[...] = e_self_d
        ssf_ref[...] = e_self_f

    @pl.when(j >= n_steps)
    def _():
        lam = _lam(lp_ref) + lam_init
        l_d = ld_ref[...]
        l_f = lfs_ref[...]
        for pp in range(P):
            g = (j - n_steps) * P + pp
            e_d = sd_ref[g]
            a = e_d[0:8] / l_d[0:8] - lam * (e_d[8:16] / l_d[8:16])
            accd_ref[...] = accd_ref[...] + _dot(a.astype(BF16), vd_refs[pp][...].astype(BF16))
            accf_ref[...] = accf_ref[...] + _dot((sf_ref[g] / l_f).astype(BF16), vf_refs[pp][...].astype(BF16))

    @pl.when(j == 2 * n_steps - 1)
    def _():
        lam = _lam(lp_ref) + lam_init
        pr = ssd_ref[...] / ld_ref[...]
        a_self = bf(pr[0:8] - lam * pr[8:16])
        od = accd_ref[...] + a_self * bf(van_ref[...])
        hsel = lax.broadcasted_iota(I32, (8, W), 1) // (2 * HEAD_DIM)
        rowi = lax.broadcasted_iota(I32, (8, W), 0)
        o = jnp.sum(jnp.where(rowi == hsel, od, 0.0), axis=0, keepdims=True)
        outs = []
        for hh in range(A_HEADS):
            og = o[:, hh * 2 * HEAD_DIM:(hh + 1) * 2 * HEAD_DIM]
            ms = jnp.mean(og * og, axis=-1, keepdims=True)
            outs.append(og * lax.rsqrt(ms + EPS) * sub_ref[...] * (1.0 - lam_init))
        oa_ref[...] = jnp.concatenate(outs, axis=1).astype(oa_ref.dtype)
        p_self = bf(ssf_ref[...] / lfs_ref[...])
        of = accf_ref[...] + p_self * bf(vfn_ref[...])
        seg8 = lax.broadcasted_iota(I32, (8, W), 1) // HEAD_DIM
        of_ref[...] = jnp.sum(jnp.where(seg8 == rowi, of, 0.0), axis=0, keepdims=True).astype(of_ref.dtype)


PAGE = 128


def _decode(layer, page_table, qa, ka_new, va_new, qf, kf_new, vf_new, lf_new_col, lam_params, subln,
            ck_a, cv_a, ck_f, cv_f, clf_t, lam_init, *, pages_per_step):
    nb, n_pages = page_table.shape
    P = pages_per_step
    n_steps = n_pages // P
    W = A_WIDTH

    def page_spec(pp, shape_tail, value_phase):
        def imap(b, j, pt):
            step = jnp.maximum(j - n_steps, 0) if value_phase else jnp.minimum(j, n_steps - 1)
            return (layer, pt[b, n_pages - 1 - (step * P + pp)], 0, 0)
        return pl.BlockSpec((None, None) + shape_tail, imap)

    rowspec = pl.BlockSpec((None, 1, W), lambda b, j, pt: (b, 0, 0))
    in_specs = [rowspec, rowspec, rowspec, rowspec, rowspec, rowspec,
                pl.BlockSpec((None, 8, 1), lambda b, j, pt: (b, 0, 0)),
                pl.BlockSpec((4, HEAD_DIM), lambda b, j, pt: (0, 0)),
                pl.BlockSpec((1, LANES), lambda b, j, pt: (0, 0))]
    args = [qa, ka_new, va_new, qf, kf_new, vf_new, lf_new_col, lam_params, subln.reshape(1, LANES)]
    for arr, tail, value_phase in ((ck_a, (PAGE, W), False), (ck_f, (PAGE, W), False),
                                   (clf_t, (F_HEADS, PAGE), False), (cv_a, (PAGE, W), True),
                                   (cv_f, (PAGE, W), True)):
        for pp in range(P):
            in_specs.append(page_spec(pp, tail, value_phase))
            args.append(arr)
    grid_spec = pltpu.PrefetchScalarGridSpec(
        num_scalar_prefetch=1,
        grid=(nb, 2 * n_steps),
        in_specs=in_specs,
        out_specs=[rowspec, rowspec],
        scratch_shapes=[pltpu.VMEM((16, W), BF16), pltpu.VMEM((8, W), BF16),
                        pltpu.VMEM((n_pages, 16, PAGE), F32), pltpu.VMEM((n_pages, 8, PAGE), F32),
                        pltpu.VMEM((16, 1), F32), pltpu.VMEM((8, 1), F32),
                        pltpu.VMEM((16, 1), F32), pltpu.VMEM((8, 1), F32),
                        pltpu.VMEM((8, W), F32), pltpu.VMEM((8, W), F32), pltpu.VMEM((8, 1), F32),
                        pltpu.VMEM((16, 1), F32), pltpu.VMEM((8, 1), F32)],
    )
    oa, of = pl.pallas_call(
        functools.partial(_decode_kernel, lam_init=lam_init, pages_per_step=P, n_steps=n_steps),
        grid_spec=grid_spec,
        out_shape=[jax.ShapeDtypeStruct((nb, 1, W), BF16), jax.ShapeDtypeStruct((nb, 1, W), BF16)],
        compiler_params=_cparams(("arbitrary", "arbitrary")),
        name="decode_attn",
    )(page_table, *args)
    return oa.reshape(nb, W), of.reshape(nb, W)


def _merge_kernel(x_ref, sh_ref, sc_ref, g1_ref, nw_ref, oa_ref, of_ref, os_ref, wg_ref, wa_ref, wf_ref, ws_ref,
                  wo_ref, o_ref):
    x = x_ref[...]
    d = x.shape[1]
    h = _norm_mod(x, nw_ref[...], sc_ref[...], sh_ref[...]).astype(BF16)
    m = None
    for j, (b_ref, w_ref) in enumerate(((oa_ref, wa_ref), (of_ref, wf_ref), (os_ref, ws_ref))):
        g = _sigmoid(_dot(h, wg_ref[:, j * d:(j + 1) * d]))
        t = g * _dot(b_ref[...], w_ref[...])
        m = t if m is None else m + t
    out = _dot(m.astype(BF16), wo_ref[...])
    o_ref[...] = x + g1_ref[...] * out


def _merge(x, shift, scale, gate, nw, oa, of, os_, wg, wa, wf, ws, wo, *, tm, tiles_per_batch, per_row_mod):
    n, d = x.shape
    if per_row_mod:
        mod_spec = pl.BlockSpec((tm, d), lambda i: (i, 0))
    else:
        mod_spec = pl.BlockSpec((None, 1, d), lambda i: (i // tiles_per_batch, 0, 0))
    row = lambda w: pl.BlockSpec((tm, w), lambda i: (i, 0))
    full = lambda a: pl.BlockSpec(a.shape, lambda i: (0,) * a.ndim)
    return pl.pallas_call(
        _merge_kernel,
        grid=(n // tm,),
        in_specs=[row(d), mod_spec, mod_spec, mod_spec, pl.BlockSpec((1, d), lambda i: (0, 0)),
                  row(A_WIDTH), row(F_WIDTH), row(S_WIDTH), full(wg), full(wa), full(wf), full(ws), full(wo)],
        out_specs=row(d),
        out_shape=jax.ShapeDtypeStruct((n, d), F32),
        compiler_params=_cparams(("arbitrary",)),
        name="merge",
    )(x, shift, scale, gate, nw, oa, of, os_, wg, wa, wf, ws, wo)


def _route_kernel(x_ref, sh_ref, sc_ref, nw_ref, rw_ref, rb_ref, h_ref, idx_ref, wt_ref):
    h = _norm_mod(x_ref[...], nw_ref[...], sc_ref[...], sh_ref[...])
    h_ref[...] = h
    tm = h.shape[0]
    logits = _dot_nt(rw_ref[...].astype(BF16), h.astype(BF16))
    scores = _sigmoid(logits)
    sel = scores + rb_ref[...]
    per = N_EXPERTS // N_EXPERT_GROUPS
    ei = lax.broadcasted_iota(I32, (N_EXPERTS, tm), 0).astype(F32)
    gid = jnp.floor(ei * (1.0 / per))
    BIG = 1e9
    gs_rows = []
    li = lax.broadcasted_iota(I32, (per, tm), 0).astype(F32)
    for g in range(N_EXPERT_GROUPS):
        sg = sel[g * per:(g + 1) * per, :]
        m1 = jnp.max(sg, axis=0, keepdims=True)
        i1 = jnp.min(jnp.where(sg == m1, li, BIG), axis=0, keepdims=True)
        m2 = jnp.max(jnp.where(li == i1, -jnp.inf, sg), axis=0, keepdims=True)
        gs_rows.append(m1 + m2)
    gs = jnp.concatenate(gs_rows, axis=0)
    gi = lax.broadcasted_iota(I32, (N_EXPERT_GROUPS, tm), 0).astype(F32)
    cur = jnp.full((N_EXPERTS, tm), NEG, F32)
    for _ in range(TOPK_GROUPS):
        m = jnp.max(gs, axis=0, keepdims=True)
        pick = jnp.min(jnp.where(gs == m, gi, BIG), axis=0, keepdims=True)
        gs = jnp.where(gi == pick, -jnp.inf, gs)
        cur = jnp.where(gid == pick, sel, cur)
    idx_rows, w_rows = [], []
    for _ in range(TOP_K):
        m = jnp.max(cur, axis=0, keepdims=True)
        pick = jnp.min(jnp.where(cur == m, ei, BIG), axis=0, keepdims=True)
        hit = ei == pick
        w_rows.append(jnp.sum(jnp.where(hit, scores, 0.0), axis=0, keepdims=True))
        idx_rows.append(pick)
        cur = jnp.where(hit, -jnp.inf, cur)
    w = jnp.concatenate(w_rows, axis=0)
    idx_ref[...] = jnp.concatenate(idx_rows, axis=0).astype(I32)
    wt_ref[...] = w / jnp.sum(w, axis=0, keepdims=True) * ROUTED_SCALE


def _route(x, shift, scale, nw, rw_t, rb_col, *, tm, tiles_per_batch, per_row_mod):
    n, d = x.shape
    if per_row_mod:
        mod_spec = pl.BlockSpec((tm, d), lambda i: (i, 0))
    else:
        mod_spec = pl.BlockSpec((None, 1, d), lambda i: (i // tiles_per_batch, 0, 0))
    row = lambda w: pl.BlockSpec((tm, w), lambda i: (i, 0))
    colspec = pl.BlockSpec((TOP_K, tm), lambda i: (0, i))
    return pl.pallas_call(
        _route_kernel,
        grid=(n // tm,),
        in_specs=[row(d), mod_spec, mod_spec, pl.BlockSpec((1, d), lambda i: (0, 0)),
                  pl.BlockSpec((N_EXPERTS, d), lambda i: (0, 0)), pl.BlockSpec((N_EXPERTS, 1), lambda i: (0, 0))],
        out_specs=[row(d), colspec, colspec],
        out_shape=[jax.ShapeDtypeStruct((n, d), F32), jax.ShapeDtypeStruct((TOP_K, n), I32),
                   jax.ShapeDtypeStruct((TOP_K, n), F32)],
        compiler_params=_cparams(("arbitrary",)),
        name="route",
    )(x, shift, scale, nw, rw_t, rb_col)


def _expert_kernel(be_ref, nu_ref, tok_hbm, x_hbm, wg_ref, wu_ref, wd_ref, y_ref, tok_smem, xbuf, isem, gsem, *, blk):
    i = pl.program_id(0)
    n_used = nu_ref[0]
    slot = i % 2

    def idx_copy(b, s):
        return pltpu.make_async_copy(tok_hbm.at[b], tok_smem.at[s], isem.at[s])

    @pl.when(i == 0)
    def _():
        idx_copy(0, 0).start()

    @pl.when(i < n_used)
    def _():
        idx_copy(i, slot).wait()

        @pl.when(i + 1 < n_used)
        def _():
            idx_copy(i + 1, 1 - slot).start()

        def issue(r, carry):
            tok = tok_smem[slot, r]
            pltpu.make_async_copy(x_hbm.at[pl.ds(tok, 1), :], xbuf.at[slot, pl.ds(r, 1), :], gsem.at[slot]).start()
            return carry

        lax.fori_loop(0, blk, issue, 0, unroll=8)

    @pl.when(jnp.logical_and(i >= 1, i - 1 < n_used))
    def _():
        ps = 1 - slot
        pltpu.make_async_copy(x_hbm.at[pl.ds(0, blk), :], xbuf.at[ps], gsem.at[ps]).wait()
        xb = xbuf[ps].astype(BF16)
        hg = _dot(xb, wg_ref[...])
        hu = _dot(xb, wu_ref[...])
        hh = (_silu(hg) * hu).astype(BF16)
        y_ref[...] = _dot(hh, wd_ref[...])

    @pl.when(jnp.logical_and(i >= 1, i - 1 >= n_used))
    def _():
        y_ref[...] = jnp.zeros_like(y_ref)


def _experts(x, slot_tok, block_e, n_used, wg, wu, wd, *, blk):
    n, d = x.shape
    n_blocks = slot_tok.shape[0]
    de = wg.shape[2]

    def wmap(i, be, nu):
        return (be[jnp.clip(i - 1, 0, n_blocks - 1)], 0, 0)

    def ymap(i, be, nu):
        return (jnp.maximum(i - 1, 0), 0)

    grid_spec = pltpu.PrefetchScalarGridSpec(
        num_scalar_prefetch=2,
        grid=(n_blocks + 1,),
        in_specs=[pl.BlockSpec(memory_space=pl.ANY), pl.BlockSpec(memory_space=pl.ANY),
                  pl.BlockSpec((None, d, de), wmap), pl.BlockSpec((None, d, de), wmap),
                  pl.BlockSpec((None, de, d), wmap)],
        out_specs=pl.BlockSpec((blk, d), ymap),
        scratch_shapes=[pltpu.SMEM((2, blk), I32), pltpu.VMEM((2, blk, d), F32),
                        pltpu.SemaphoreType.DMA((2,)), pltpu.SemaphoreType.DMA((2,))],
    )
    return pl.pallas_call(
        functools.partial(_expert_kernel, blk=blk),
        grid_spec=grid_spec,
        out_shape=jax.ShapeDtypeStruct((n_blocks * blk, d), F32),
        compiler_params=_cparams(("arbitrary",)),
        name="experts",
    )(block_e, n_used, slot_tok, x, wg, wu, wd)


def _combine_kernel(pos_hbm, y_hbm, x_ref, h_ref, g2_ref, wt_ref, sg_ref, su_ref, sd_ref, fin_ref, o_ref,
                    pos_smem, ybuf, isem, gsem, *, tc, n_tiles, final):
    i = pl.program_id(0)
    slot = i % 2

    def idx_copy(t, s):
        return pltpu.make_async_copy(pos_hbm.at[t], pos_smem.at[s], isem.at[s])

    @pl.when(i == 0)
    def _():
        idx_copy(0, 0).start()

    @pl.when(i < n_tiles)
    def _():
        idx_copy(i, slot).wait()

        @pl.when(i + 1 < n_tiles)
        def _():
            idx_copy(i + 1, 1 - slot).start()

        def issue(t, carry):
            for k in range(TOP_K):
                p = pos_smem[slot, t * TOP_K + k]
                pltpu.make_async_copy(y_hbm.at[pl.ds(p, 1), :], ybuf.at[slot, k, pl.ds(t, 1), :],
                                      gsem.at[slot]).start()
            return carry

        lax.fori_loop(0, tc, issue, 0, unroll=2)

    @pl.when(i >= 1)
    def _():
        ps = 1 - slot
        hb = h_ref[...].astype(BF16)
        sh = (_silu(_dot(hb, sg_ref[...])) * _dot(hb, su_ref[...])).astype(BF16)
        acc = _dot(sh, sd_ref[...])
        for k in range(TOP_K):
            pltpu.make_async_copy(y_hbm.at[pl.ds(0, tc), :], ybuf.at[ps, k], gsem.at[ps]).wait()
        w = wt_ref[...]
        for k in range(TOP_K):
            acc = acc + w[:, k:k + 1] * ybuf[ps, k]
        xo = x_ref[...] + g2_ref[...] * acc
        if final:
            ms = jnp.mean(xo * xo, axis=-1, keepdims=True)
            xo = xo * lax.rsqrt(ms + EPS) * fin_ref[...]
        o_ref[...] = xo


def _combine(pos_tiles, y_sorted, x, h, gate, wts, sg, su, sd, fin_w, *, tc, tiles_per_batch, per_row_mod, final):
    n, d = x.shape
    n_tiles = n // tc
    prev = lambda i: jnp.maximum(i - 1, 0)
    if per_row_mod:
        mod_spec = pl.BlockSpec((tc, d), lambda i: (prev(i), 0))
    else:
        mod_spec = pl.BlockSpec((None, 1, d), lambda i: (prev(i) // tiles_per_batch, 0, 0))
    row = lambda w: pl.BlockSpec((tc, w), lambda i: (prev(i), 0))
    full = lambda a: pl.BlockSpec(a.shape, lambda i: (0,) * a.ndim)
    return pl.pallas_call(
        functools.partial(_combine_kernel, tc=tc, n_tiles=n_tiles, final=final),
        grid=(n_tiles + 1,),
        in_specs=[pl.BlockSpec(memory_space=pl.ANY), pl.BlockSpec(memory_space=pl.ANY),
                  row(d), row(d), mod_spec, row(TOP_K), full(sg), full(su), full(sd),
                  pl.BlockSpec((1, d), lambda i: (0, 0))],
        out_specs=row(d),
        out_shape=jax.ShapeDtypeStruct((n, d), F32),
        scratch_shapes=[pltpu.SMEM((2, tc * TOP_K), I32), pltpu.VMEM((2, TOP_K, tc, d), F32),
                        pltpu.SemaphoreType.DMA((2,)), pltpu.SemaphoreType.DMA((2,))],
        compiler_params=_cparams(("arbitrary",)),
        name="combine",
    )(pos_tiles, y_sorted, x, h, gate, wts, sg, su, sd, fin_w)


def _dispatch_indices(idx, blk):
    n_tok = idx.shape[0]
    n_rows = n_tok * TOP_K
    n_blocks = -(-n_rows // blk) + N_EXPERTS
    flat_e = idx.reshape(-1)
    order = jnp.argsort(flat_e)
    sorted_e = flat_e[order]
    counts = jnp.bincount(flat_e, length=N_EXPERTS)
    padded = (counts + blk - 1) // blk * blk
    start = jnp.cumsum(counts) - counts
    pad_end = jnp.cumsum(padded)
    pad_start = pad_end - padded
    dest = (pad_start[sorted_e] + jnp.arange(n_rows) - start[sorted_e]).astype(I32)
    slot_tok = jnp.zeros((n_blocks * blk,), I32).at[dest].set((order // TOP_K).astype(I32))
    pos = jnp.zeros((n_rows,), I32).at[order].set(dest)
    block_e = jnp.minimum(jnp.searchsorted(pad_end, jnp.arange(n_blocks) * blk, side='right'),
                          N_EXPERTS - 1).astype(I32)
    n_used = (pad_end[-1] // blk).astype(I32).reshape(1)
    return slot_tok.reshape(n_blocks, blk), pos, block_e, n_used


def _moe(x1, mod, nw, lw, fin_w, *, tm, tc, blk, tiles_per_batch_tm, tiles_per_batch_tc, per_row_mod, final):
    shift2, scale2, gate2 = mod
    h2, idx_t, wts_t = _route(x1, shift2, scale2, nw, lw['router_w_t'], lw['router_b'], tm=tm,
                              tiles_per_batch=tiles_per_batch_tm, per_row_mod=per_row_mod)
    idx = idx_t.T
    wts = wts_t.T
    slot_tok, pos, block_e, n_used = _dispatch_indices(idx, blk)
    y_sorted = _experts(h2, slot_tok, block_e, n_used, lw['exp_g'], lw['exp_u'], lw['exp_d'], blk=blk)
    n = x1.shape[0]
    pos_tiles = pos.reshape(n // tc, tc * TOP_K)
    return _combine(pos_tiles, y_sorted, x1, h2, gate2, wts, lw['sh_g'], lw['sh_u'], lw['sh_d'], fin_w,
                    tc=tc, tiles_per_batch=tiles_per_batch_tc, per_row_mod=per_row_mod, final=final)


def _pack_w_in(w_in_l):
    d = w_in_l.shape[0]
    cuts = [0]
    for wdt in (A_WIDTH, A_WIDTH, A_WIDTH, F_WIDTH, F_WIDTH, F_WIDTH, F_HEADS, S_WIDTH, CONV_DIM, S_HEADS,
                N_BRANCH * d):
        cuts.append(cuts[-1] + wdt)
    seg = lambda k: w_in_l[:, cuts[k]:cuts[k + 1]]
    small = jnp.concatenate([seg(6), seg(9), jnp.zeros((d, SMALL_W - F_HEADS - S_HEADS), w_in_l.dtype)], axis=1)
    pack = jnp.concatenate([seg(0), seg(1), seg(2), seg(3), seg(4), seg(5), seg(7), seg(8), small], axis=1)
    return pack.astype(BF16), seg(10).astype(BF16)


def _rope_tables(pos):
    half = ROT_DIM // 2
    inv = ROPE_THETA ** (-jnp.arange(half, dtype=F32) / half)
    ang = pos.astype(F32)[:, None] * inv[None, :]
    cos, sin = jnp.cos(ang), jnp.sin(ang)
    n = pos.shape[0]
    one = jnp.ones((n, HEAD_DIM - ROT_DIM), F32)
    zero_h = jnp.zeros((n, half), F32)
    zero_r = jnp.zeros((n, HEAD_DIM - ROT_DIM), F32)
    rc = jnp.concatenate([cos, cos, one], axis=1)
    rn = jnp.concatenate([-sin, zero_h, zero_r], axis=1)
    rp = jnp.concatenate([zero_h, sin, zero_r], axis=1)
    rep = LANES // HEAD_DIM
    return tuple(jnp.tile(t, (1, rep)) for t in (rc, rn, rp))


def kernel(x_prompt, x_sample, cache_diff_k, cache_diff_v, cache_fox_k, cache_fox_v, cache_fox_logf, state_ssm, state_conv, page_table, c_prompt, c_sample, w_ada, b_ada, norm_mix, norm_ffn, w_in, diff_lambda, diff_subln, fox_b_f, ssm_conv_w, ssm_conv_b, ssm_dt_bias, ssm_a_log, ssm_d, ssm_norm, w_branch_a, w_branch_f, w_branch_s, w_out, router_w, router_bias, exp_w_gate, exp_w_up, exp_w_down, shared_w_gate, shared_w_up, shared_w_down, norm_final):
    nbp, seq, d = x_prompt.shape
    nbs, tdec, _ = x_sample.shape
    depth = w_in.shape[0]
    n_phys, page = cache_diff_k.shape[1], cache_diff_k.shape[2]
    n_pages = page_table.shape[1]
    assert tdec == 1 and page == PAGE
    n_p = nbp * seq
    n_s = nbs

    tm = min(256, seq)
    tq = min(256, seq)
    chunk = min(128, seq)
    tc_p = min(128, seq)
    blk_p = 256
    blk_s = 16
    pps = next(c for c in (4, 2, 1) if n_pages % c == 0)

    xp = x_prompt.reshape(n_p, d)
    xs = x_sample.reshape(n_s, d)
    ck_a = cache_diff_k.reshape(depth, n_phys, page, A_WIDTH)
    cv_a = cache_diff_v.reshape(depth, n_phys, page, A_WIDTH)
    ck_f = cache_fox_k.reshape(depth, n_phys, page, F_WIDTH)
    cv_f = cache_fox_v.reshape(depth, n_phys, page, F_WIDTH)
    clf_t = jnp.swapaxes(cache_fox_logf, 2, 3)

    tabs_p = _rope_tables(jnp.arange(seq))
    tabs_s = _rope_tables(jnp.full((n_s,), n_pages * page, I32))

    n_c = nbp + nbs
    n_c_pad = -(-n_c // 8) * 8
    c_all = jnp.concatenate([c_prompt, c_sample, jnp.zeros((n_c_pad - n_c, d), F32)], axis=0)

    st_p, st_s = [], []
    y_p = y_s = None
    fin_w = norm_final.reshape(1, d)
    for l in range(depth):
        lam_init = 0.8 - 0.6 * math.exp(-0.3 * l)
        w_pack, w_gates = _pack_w_in(w_in[l])
        bias_small = jnp.zeros((1, SMALL_W), F32).at[0, :F_HEADS].set(fox_b_f[l])
        lw = {'router_w_t': router_w[l].T, 'router_b': router_bias[l].reshape(N_EXPERTS, 1),
              'exp_g': exp_w_gate[l].astype(BF16), 'exp_u': exp_w_up[l].astype(BF16),
              'exp_d': exp_w_down[l].astype(BF16), 'sh_g': shared_w_gate[l].astype(BF16),
              'sh_u': shared_w_up[l].astype(BF16), 'sh_d': shared_w_down[l].astype(BF16)}
        wa, wf, ws = (w_branch_a[l].astype(BF16), w_branch_f[l].astype(BF16), w_branch_s[l].astype(BF16))
        wo = w_out[l].astype(BF16)
        sp = {'ssm_conv_w': ssm_conv_w[l], 'ssm_conv_b': ssm_conv_b[l], 'ssm_dt_bias': ssm_dt_bias[l],
              'ssm_a_log': ssm_a_log[l], 'ssm_d': ssm_d[l], 'ssm_norm': ssm_norm[l]}
        nm = norm_mix[l].reshape(1, d)
        nf = norm_ffn[l].reshape(1, d)
        final = l == depth - 1

        mod = _ada(c_all, w_ada[l], b_ada[l])
        mod6 = [mod[:, k * d:(k + 1) * d] for k in range(6)]
        mod_p = [m[:nbp].reshape(nbp, 1, d) for m in mod6]
        mod_s = [m[nbp:nbp + nbs] for m in mod6]

        (qa, ka, kab, va, vab, qf, kf, kfb, vf, vfb, sz, sxbc, small, lf, cum) = _proj(
            xp, mod_p[0], mod_p[1], nm, w_pack, bias_small, tabs_p, tm=tm, tiles_per_batch=seq // tm,
            per_row_mod=False)
        o_a = _diff_attn(qa, kab, vab, diff_lambda[l], diff_subln[l], lam_init, nb=nbp, seq=seq, tq=tq)
        cum_t = jnp.swapaxes(cum[:, :F_HEADS].reshape(nbp, seq // tq, tq, F_HEADS), 2, 3)
        o_f = _fox_attn(qf, kfb, vfb, cum, cum_t, nb=nbp, seq=seq, tq=tq)
        o_s, h_fin, conv_p = _ssd_prompt(sxbc, sz, small, sp, nb=nbp, seq=seq, chunk=chunk)
        x1 = _merge(xp, mod_p[0], mod_p[1], mod_p[2], nm, o_a, o_f, o_s, w_gates, wa, wf, ws, wo, tm=tm,
                    tiles_per_batch=seq // tm, per_row_mod=False)
        xp = _moe(x1, mod_p[3:6], nf, lw, fin_w, tm=tm, tc=tc_p, blk=blk_p, tiles_per_batch_tm=seq // tm,
                  tiles_per_batch_tc=seq // tc_p, per_row_mod=False, final=final)
        st_p.append((ka.reshape(nbp, seq, A_HEADS, 2, HEAD_DIM), va.reshape(nbp, seq, A_HEADS, 2 * HEAD_DIM),
                     kf.reshape(nbp, seq, F_HEADS, HEAD_DIM), vf.reshape(nbp, seq, F_HEADS, HEAD_DIM),
                     lf[:, :F_HEADS].reshape(nbp, seq, F_HEADS), h_fin, conv_p))

        (qa, ka, kab, va, vab, qf, kf, kfb, vf, vfb, sz, sxbc, small, lf, cum) = _proj(
            xs, mod_s[0], mod_s[1], nm, w_pack, bias_small, tabs_s, tm=n_s, tiles_per_batch=1, per_row_mod=True)
        lf_new_col = lf[:, :F_HEADS].reshape(n_s, F_HEADS, 1)
        r3 = lambda a: a.reshape(n_s, 1, a.shape[1])
        o_a, o_f = _decode(l, page_table, r3(qa), r3(ka), r3(va), r3(qf), r3(kf), r3(vf), lf_new_col,
                           diff_lambda[l], diff_subln[l], ck_a, cv_a, ck_f, cv_f, clf_t, lam_init,
                           pages_per_step=pps)
        o_s, h_new, conv_s = _ssd_sample(sxbc, state_conv[l], sz, small, state_ssm[l], sp)
        x1 = _merge(xs, mod_s[0], mod_s[1], mod_s[2], nm, o_a, o_f, o_s, w_gates, wa, wf, ws, wo, tm=n_s,
                    tiles_per_batch=1, per_row_mod=True)
        xs = _moe(x1, mod_s[3:6], nf, lw, fin_w, tm=n_s, tc=n_s, blk=blk_s, tiles_per_batch_tm=1,
                  tiles_per_batch_tc=1, per_row_mod=True, final=final)
        st_s.append((ka.reshape(n_s, 1, A_HEADS, 2, HEAD_DIM), va.reshape(n_s, 1, A_HEADS, 2 * HEAD_DIM),
                     kf.reshape(n_s, 1, F_HEADS, HEAD_DIM), vf.reshape(n_s, 1, F_HEADS, HEAD_DIM),
                     lf[:, :F_HEADS].reshape(n_s, 1, F_HEADS), h_new, conv_s))

    y_p = xp.reshape(nbp, seq, d)
    y_s = xs.reshape(nbs, 1, d)
    stk = lambda states, i: jnp.stack([s[i] for s in states])
    return (y_p, y_s,
            stk(st_p, 0), stk(st_s, 0), stk(st_p, 1), stk(st_s, 1),
            stk(st_p, 2), stk(st_s, 2), stk(st_p, 3), stk(st_s, 3),
            stk(st_p, 4), stk(st_s, 4), stk(st_p, 5), stk(st_s, 5),
            stk(st_p, 6), stk(st_s, 6))
```

```python
import functools
import math

import jax
import jax.numpy as jnp
import numpy as np
from jax import lax
from jax.experimental import pallas as pl
from jax.experimental.pallas import tpu as pltpu

F32 = jnp.float32
BF16 = jnp.bfloat16
I32 = jnp.int32

HEAD_DIM = 64
SCALE = HEAD_DIM ** -0.5
A_HEADS = 4
F_HEADS = 8
S_HEADS = 8
S_HEADDIM = 64
S_GROUPS = 2
HPG = S_HEADS // S_GROUPS
D_STATE = 64
CONV_W = 4
ROT_DIM = HEAD_DIM // 4
ROPE_THETA = 500000.0
A_WIDTH = A_HEADS * 2 * HEAD_DIM
F_WIDTH = F_HEADS * HEAD_DIM
S_WIDTH = S_HEADS * S_HEADDIM
CONV_DIM = S_WIDTH + 2 * S_GROUPS * D_STATE
N_BRANCH = 3
N_EXPERTS = 64
N_EXPERT_GROUPS = 8
TOPK_GROUPS = 4
TOP_K = 8
ROUTED_SCALE = 2.5
EPS = 1e-6
NEG = -1e30
LANES = 128
SMALL_W = LANES
VMEM_LIMIT = 56 * 1024 * 1024

_OFF = {}
_o = 0
for _n, _w in (("aq", A_WIDTH), ("ak", A_WIDTH), ("av", A_WIDTH), ("fq", F_WIDTH), ("fk", F_WIDTH),
               ("fv", F_WIDTH), ("sz", S_WIDTH), ("sxbc", CONV_DIM), ("small", SMALL_W)):
    _OFF[_n] = (_o, _o + _w)
    _o += _w
PACK_COLS = _o


def _cparams(sem):
    return pltpu.CompilerParams(dimension_semantics=sem, vmem_limit_bytes=VMEM_LIMIT)


def _dot(a, b):
    return jnp.dot(a, b, preferred_element_type=F32)


def _dot_nt(a, b):
    return lax.dot_general(a, b, (((1,), (1,)), ((), ())), preferred_element_type=F32)


def _silu(x):
    return x * (1.0 / (1.0 + jnp.exp(-x)))


def _sigmoid(x):
    return 1.0 / (1.0 + jnp.exp(-x))


def _norm_mod(x, nw, scale, shift):
    ms = jnp.mean(x * x, axis=-1, keepdims=True)
    y = x * lax.rsqrt(ms + EPS) * nw
    return y * (1.0 + scale) + shift


def _ada_kernel(c_ref, w_ref, b_ref, o_ref):
    c = _silu(c_ref[...]).astype(BF16)
    o_ref[...] = _dot(c, w_ref[...].astype(BF16)) + b_ref[...]


def _ada(c_all, w, b, tn=512):
    m, d = c_all.shape
    n = w.shape[1]
    return pl.pallas_call(
        _ada_kernel,
        grid=(n // tn,),
        in_specs=[pl.BlockSpec((m, d), lambda j: (0, 0)),
                  pl.BlockSpec((d, tn), lambda j: (0, j)),
                  pl.BlockSpec((1, tn), lambda j: (0, j))],
        out_specs=pl.BlockSpec((m, tn), lambda j: (0, j)),
        out_shape=jax.ShapeDtypeStruct((m, n), F32),
        compiler_params=_cparams(("arbitrary",)),
        name="ada",
    )(c_all, w, b.reshape(1, n))


def _proj_kernel(x_ref, sh_ref, sc_ref, nw_ref, w_ref, bias_ref, rc_ref, rn_ref, rp_ref,
                 qa_ref, ka_ref, kab_ref, va_ref, vab_ref,
                 qf_ref, kf_ref, kfb_ref, vf_ref, vfb_ref,
                 sz_ref, sxbc_ref, small_ref, lf_ref, cum_ref, run_ref, *, tiles_per_batch):
    i = pl.program_id(0)
    h = _norm_mod(x_ref[...], nw_ref[...], sc_ref[...], sh_ref[...]).astype(BF16)
    tm = h.shape[0]

    def col(name):
        a, b = _OFF[name]
        return _dot(h, w_ref[:, a:b])

    rc = jnp.concatenate([rc_ref[...]] * (A_WIDTH // LANES), axis=1)
    rn = jnp.concatenate([rn_ref[...]] * (A_WIDTH // LANES), axis=1)
    rp = jnp.concatenate([rp_ref[...]] * (A_WIDTH // LANES), axis=1)

    def rope(v):
        half = ROT_DIM // 2
        parts = []
        for j in range(A_WIDTH // LANES):
            vj = v[:, j * LANES:(j + 1) * LANES]
            parts.append((pltpu.roll(vj, LANES - half, axis=1), pltpu.roll(vj, half, axis=1)))
        left = jnp.concatenate([p[0] for p in parts], axis=1)
        right = jnp.concatenate([p[1] for p in parts], axis=1)
        return v * rc + left * rn + right * rp

    aq = rope(col("aq"))
    qa_ref[...] = (aq * SCALE).astype(BF16)
    ak = rope(col("ak"))
    ka_ref[...] = ak
    kab_ref[...] = ak.astype(BF16)
    av = col("av")
    va_ref[...] = av
    vab_ref[...] = av.astype(BF16)
    qf_ref[...] = (col("fq") * SCALE).astype(BF16)
    fk = col("fk")
    kf_ref[...] = fk
    kfb_ref[...] = fk.astype(BF16)
    fv = col("fv")
    vf_ref[...] = fv
    vfb_ref[...] = fv.astype(BF16)
    sz_ref[...] = col("sz")
    sxbc_ref[...] = col("sxbc")
    small = col("small")
    small_ref[...] = small
    z = small + bias_ref[...]
    lf = jnp.minimum(z, 0.0) - jnp.log(1.0 + jnp.exp(-jnp.abs(z)))
    lf_ref[...] = lf

    @pl.when(i % tiles_per_batch == 0)
    def _():
        run_ref[...] = jnp.zeros_like(run_ref)

    r = lax.broadcasted_iota(I32, (tm, tm), 0)
    c = lax.broadcasted_iota(I32, (tm, tm), 1)
    tri = jnp.where(c <= r, 1.0, 0.0).astype(F32)
    cum = jnp.dot(tri, lf, preferred_element_type=F32, precision=lax.Precision.HIGHEST) + run_ref[...]
    cum_ref[...] = cum
    run_ref[...] = cum[tm - 1:tm, :]


def _proj(x, shift, scale, nw, w_pack, bias_small, rope_tabs, *, tm, tiles_per_batch, per_row_mod):
    n, d = x.shape
    nt = n // tm
    if per_row_mod:
        mod_spec = pl.BlockSpec((tm, d), lambda i: (i, 0))
    else:
        mod_spec = pl.BlockSpec((None, 1, d), lambda i: (i // tiles_per_batch, 0, 0))
    tab_spec = pl.BlockSpec((tm, LANES), lambda i: (i % tiles_per_batch, 0))
    row = lambda w: pl.BlockSpec((tm, w), lambda i: (i, 0))
    f = lambda w, dt: jax.ShapeDtypeStruct((n, w), dt)
    out_shapes = [f(A_WIDTH, BF16), f(A_WIDTH, F32), f(A_WIDTH, BF16), f(A_WIDTH, F32), f(A_WIDTH, BF16),
                  f(F_WIDTH, BF16), f(F_WIDTH, F32), f(F_WIDTH, BF16), f(F_WIDTH, F32), f(F_WIDTH, BF16),
                  f(S_WIDTH, F32), f(CONV_DIM, F32), f(SMALL_W, F32), f(SMALL_W, F32), f(SMALL_W, F32)]
    out_specs = [row(s.shape[1]) for s in out_shapes]
    return pl.pallas_call(
        functools.partial(_proj_kernel, tiles_per_batch=tiles_per_batch),
        grid=(nt,),
        in_specs=[row(d), mod_spec, mod_spec, pl.BlockSpec((1, d), lambda i: (0, 0)),
                  pl.BlockSpec((d, PACK_COLS), lambda i: (0, 0)),
                  pl.BlockSpec((1, SMALL_W), lambda i: (0, 0)),
                  tab_spec, tab_spec, tab_spec],
        out_specs=out_specs,
        out_shape=out_shapes,
        scratch_shapes=[pltpu.VMEM((1, SMALL_W), F32)],
        compiler_params=_cparams(("arbitrary",)),
        name="proj",
    )(x, shift, scale, nw, w_pack, bias_small, *rope_tabs)


_ROW_OFF = {}
_o = 0
for _n, _w in (("ak", A_WIDTH), ("av", A_WIDTH), ("fk", F_WIDTH), ("sz", S_WIDTH), ("sxbc", CONV_DIM),
               ("small", SMALL_W)):
    _ROW_OFF[_n] = (_o, _o + _w)
    _o += _w
ROW_COLS = _o
_T_OFF = {n: (j * A_WIDTH, (j + 1) * A_WIDTH) for j, n in enumerate(("aq", "ak", "av", "fq", "fk", "fv"))}
T_ROWS = 6 * A_WIDTH


def _proj_t_kernel(x_ref, sh_ref, sc_ref, nw_ref, wr_ref, wt_ref, bias_ref, rc_ref, rn_ref, rp_ref,
                   rct_ref, rnt_ref, rpt_ref, sel_ref,
                   qat_ref, kab_ref, kat_ref, va_ref, vat_ref,
                   qft_ref, kfb_ref, ckf_ref, kft_ref, vft32_ref, vft_ref,
                   sz_ref, sxbc_ref, small_ref, lf_ref, cum_ref, run_ref, *, tiles_per_batch):
    i = pl.program_id(0)
    h = _norm_mod(x_ref[...], nw_ref[...], sc_ref[...], sh_ref[...]).astype(BF16)
    tm = h.shape[0]
    half = ROT_DIM // 2
    reps = A_WIDTH // LANES

    def col(name):
        a, b = _ROW_OFF[name]
        return _dot(h, wr_ref[:, a:b])

    def colt(name):
        a, b = _T_OFF[name]
        return _dot_nt(wt_ref[a:b, :], h)

    rc = jnp.concatenate([rc_ref[...]] * reps, axis=1)
    rn = jnp.concatenate([rn_ref[...]] * reps, axis=1)
    rp = jnp.concatenate([rp_ref[...]] * reps, axis=1)
    rct = jnp.concatenate([rct_ref[...]] * reps, axis=0)
    rnt = jnp.concatenate([rnt_ref[...]] * reps, axis=0)
    rpt = jnp.concatenate([rpt_ref[...]] * reps, axis=0)

    def rope(v):
        parts = []
        for j in range(reps):
            vj = v[:, j * LANES:(j + 1) * LANES]
            parts.append((pltpu.roll(vj, LANES - half, axis=1), pltpu.roll(vj, half, axis=1)))
        left = jnp.concatenate([p[0] for p in parts], axis=1)
        right = jnp.concatenate([p[1] for p in parts], axis=1)
        return v * rc + left * rn + right * rp

    def rope_t(v):
        up = jnp.concatenate([v[half:], v[:half]], axis=0)
        down = jnp.concatenate([v[-half:], v[:-half]], axis=0)
        return v * rct + up * rnt + down * rpt

    qat_ref[...] = (rope_t(colt("aq")) * SCALE).astype(BF16)
    kab_ref[...] = rope(col("ak")).astype(BF16)
    kat_ref[...] = rope_t(colt("ak"))
    va_ref[...] = col("av")
    vat_ref[...] = colt("av").astype(BF16)
    qft_ref[...] = (colt("fq") * SCALE).astype(BF16)
    kfb_ref[...] = col("fk").astype(BF16)
    kft_ref[...] = colt("fk")
    fvt = colt("fv")
    vft32_ref[...] = fvt
    vft_ref[...] = fvt.astype(BF16)
    sz_ref[...] = col("sz")
    sxbc_ref[...] = col("sxbc")
    small = col("small")
    small_ref[...] = small
    z = small + bias_ref[...]
    lf = jnp.minimum(z, 0.0) - jnp.log(1.0 + jnp.exp(-jnp.abs(z)))
    lf_ref[...] = lf

    @pl.when(i % tiles_per_batch == 0)
    def _():
        run_ref[...] = jnp.zeros_like(run_ref)

    r = lax.broadcasted_iota(I32, (tm, tm), 0)
    c = lax.broadcasted_iota(I32, (tm, tm), 1)
    tri = jnp.where(c <= r, 1.0, 0.0).astype(F32)
    cum = jnp.dot(tri, lf, preferred_element_type=F32, precision=lax.Precision.HIGHEST) + run_ref[...]
    cum_ref[...] = cum
    run_ref[...] = cum[tm - 1:tm, :]
    hi = cum.astype(BF16)
    r1 = cum - hi.astype(F32)
    mid = r1.astype(BF16)
    lo = (r1 - mid.astype(F32)).astype(BF16)
    ckf_ref[...] = _dot(jnp.concatenate([hi, mid, lo], axis=1), sel_ref[...]).astype(BF16)


def _ck_select():
    sel = np.zeros((CK_PIECES * LANES, F_WIDTH), np.float32)
    for piece in range(CK_PIECES):
        for hd in range(F_HEADS):
            sel[piece * LANES + hd, (hd // 2) * LANES + (hd % 2) * CK_PIECES + piece] = 1.0
    return jnp.asarray(sel, BF16)


def _proj_t(x, shift, scale, nw, w_row, w_t, bias_small, rope_tabs, rope_tabs_t, sel, *, nb, seq, tm):
    n, d = x.shape
    nt = n // tm
    tpb = seq // tm
    mod_spec = pl.BlockSpec((None, 1, d), lambda i: (i // tpb, 0, 0))
    tab_spec = pl.BlockSpec((tm, LANES), lambda i: (i % tpb, 0))
    tabt_spec = pl.BlockSpec((LANES, tm), lambda i: (0, i % tpb))
    row = lambda w: pl.BlockSpec((tm, w), lambda i: (i, 0))
    tile_t = pl.BlockSpec((None, A_WIDTH, tm), lambda i: (i, 0, 0))
    leaf_t = pl.BlockSpec((None, A_WIDTH, tm), lambda i: (i // tpb, 0, i % tpb))
    full = lambda a: pl.BlockSpec(a.shape, lambda i: (0,) * a.ndim)
    f = lambda w, dt: jax.ShapeDtypeStruct((n, w), dt)
    tiles = lambda dt: jax.ShapeDtypeStruct((nt, A_WIDTH, tm), dt)
    leaf = jax.ShapeDtypeStruct((nb, A_WIDTH, seq), F32)
    out_shapes = [tiles(BF16), f(A_WIDTH, BF16), leaf, f(A_WIDTH, F32), tiles(BF16),
                  tiles(BF16), f(F_WIDTH, BF16), f(F_WIDTH, BF16), leaf, leaf, tiles(BF16),
                  f(S_WIDTH, F32), f(CONV_DIM, F32), f(SMALL_W, F32), f(SMALL_W, F32), f(SMALL_W, F32)]
    out_specs = [tile_t, row(A_WIDTH), leaf_t, row(A_WIDTH), tile_t,
                 tile_t, row(F_WIDTH), row(F_WIDTH), leaf_t, leaf_t, tile_t,
                 row(S_WIDTH), row(CONV_DIM), row(SMALL_W), row(SMALL_W), row(SMALL_W)]
    return pl.pallas_call(
        functools.partial(_proj_t_kernel, tiles_per_batch=tpb),
        grid=(nt,),
        in_specs=[row(d), mod_spec, mod_spec, pl.BlockSpec((1, d), lambda i: (0, 0)), full(w_row), full(w_t),
                  pl.BlockSpec((1, SMALL_W), lambda i: (0, 0)), tab_spec, tab_spec, tab_spec,
                  tabt_spec, tabt_spec, tabt_spec, full(sel)],
        out_specs=out_specs,
        out_shape=out_shapes,
        scratch_shapes=[pltpu.VMEM((1, SMALL_W), F32)],
        compiler_params=_cparams(("arbitrary",)),
        name="proj_t",
    )(x, shift, scale, nw, w_row, w_t, bias_small, *rope_tabs, *rope_tabs_t, sel)


def _lam(lp_ref):
    lp = lp_ref[...]
    s01 = jnp.sum(lp[0:1, :] * lp[1:2, :], axis=-1, keepdims=True)
    s23 = jnp.sum(lp[2:3, :] * lp[3:4, :], axis=-1, keepdims=True)
    return jnp.exp(s01) - jnp.exp(s23)


CK_PIECES = 3


def _attn_kernel(qt_ref, k_ref, *rest, fox, lam_init, tq):
    if fox:
        ckf_ref, vt_ref, o_ref, q2_ref, m_ref, l_ref, acc_ref = rest
    else:
        vt_ref, lp_ref, sub_ref, o_ref, q2_ref, m_ref, l_ref, acc_ref = rest
    qi = pl.program_id(2)
    tk = tq
    q32 = qt_ref[...].astype(F32)
    frow = lax.broadcasted_iota(I32, (LANES, tq), 0)
    q2_ref[0:LANES, :] = jnp.concatenate([jnp.where(frow < HEAD_DIM, q32, 0.0),
                                          jnp.where(frow >= HEAD_DIM, q32, 0.0)], axis=1).astype(BF16)
    if fox:
        er = lax.broadcasted_iota(I32, (LANES, 2 * tq), 0)
        ec = lax.broadcasted_iota(I32, (LANES, 2 * tq), 1)
        part = jnp.where(ec < tq, 0, 1)
        hit = jnp.logical_and(er >= part * CK_PIECES, er < (part + 1) * CK_PIECES)
        q2_ref[LANES:2 * LANES, :] = jnp.where(hit, -1.0, 0.0).astype(BF16)
    m_ref[...] = jnp.full(m_ref.shape, -jnp.inf, F32)
    l_ref[...] = jnp.zeros(l_ref.shape, F32)
    acc_ref[...] = jnp.zeros(acc_ref.shape, F32)

    def step(k0, j0, nblk, masked):
        rows = nblk * tk
        kb = k_ref[pl.ds(k0, rows), :]
        if fox:
            kb = jnp.concatenate([kb, ckf_ref[pl.ds(k0, rows), :]], axis=1)
        s = _dot(kb, q2_ref[...])
        if masked:
            r = lax.broadcasted_iota(I32, (rows, 2 * tq), 0)
            c = lax.broadcasted_iota(I32, (rows, 2 * tq), 1)
            c = jnp.where(c >= tq, c - tq, c)
            s = jnp.where(r <= c, s, NEG)
        m_old = m_ref[...]
        m_new = jnp.maximum(m_old, jnp.max(s, axis=0, keepdims=True))
        alpha = jnp.exp(m_old - m_new)
        p = jnp.exp(s - m_new)
        l_ref[...] = alpha * l_ref[...] + jnp.sum(p, axis=0, keepdims=True)
        vt = vt_ref[j0] if nblk == 1 else jnp.concatenate([vt_ref[j0 + t] for t in range(nblk)], axis=1)
        acc_ref[...] = alpha * acc_ref[...] + _dot(vt, p.astype(BF16))
        m_ref[...] = m_new

    def body(jj, carry):
        step(pl.multiple_of(jj * 2 * tk, 2 * tk), 2 * jj, 2, False)
        return carry

    lax.fori_loop(0, qi // 2, body, 0)

    @pl.when(qi % 2 == 1)
    def _():
        step(pl.multiple_of((qi - 1) * tk, tk), qi - 1, 1, False)

    step(pl.multiple_of(qi * tk, tk), qi, 1, True)

    o = acc_ref[...] / l_ref[...]
    if fox:
        ot = jnp.concatenate([o[0:HEAD_DIM, 0:tq], o[HEAD_DIM:LANES, tq:2 * tq]], axis=0)
    else:
        lam = _lam(lp_ref) + lam_init
        ot = o[:, 0:tq] - lam * o[:, tq:2 * tq]
        ms = jnp.mean(ot * ot, axis=0, keepdims=True)
        ot = ot * lax.rsqrt(ms + EPS)
    out = ot.T
    if not fox:
        out = out * sub_ref[...] * (1.0 - lam_init)
    o_ref[...] = out.astype(o_ref.dtype)


def _attn(qt, k, ckf, vt, lam_params, subln, lam_init, *, fox, nb, seq, tq):
    n = k.shape[0]
    nq = seq // tq
    groups = k.shape[1] // LANES
    qspec = pl.BlockSpec((None, LANES, tq), lambda b, h, i: (b * nq + i, h, 0))
    kspec = pl.BlockSpec((seq, LANES), lambda b, h, i: (b, h))
    vspec = pl.BlockSpec((nq, LANES, tq), lambda b, h, i: (b, h, 0))
    ospec = pl.BlockSpec((tq, LANES), lambda b, h, i: (b * nq + i, h))
    if fox:
        in_specs, args = [qspec, kspec, kspec, vspec], (qt, k, ckf, vt)
    else:
        in_specs = [qspec, kspec, vspec, pl.BlockSpec((4, HEAD_DIM), lambda b, h, i: (0, 0)),
                    pl.BlockSpec((1, LANES), lambda b, h, i: (0, 0))]
        args = (qt, k, vt, lam_params, subln.reshape(1, LANES))
    return pl.pallas_call(
        functools.partial(_attn_kernel, fox=fox, lam_init=lam_init, tq=tq),
        grid=(nb, groups, nq),
        in_specs=in_specs,
        out_specs=ospec,
        out_shape=jax.ShapeDtypeStruct((n, groups * LANES), BF16),
        scratch_shapes=[pltpu.VMEM(((2 if fox else 1) * LANES, 2 * tq), BF16),
                        pltpu.VMEM((1, 2 * tq), F32), pltpu.VMEM((1, 2 * tq), F32),
                        pltpu.VMEM((LANES, 2 * tq), F32)],
        compiler_params=_cparams(("arbitrary", "arbitrary", "arbitrary")),
        name="fox_attn" if fox else "diff_attn",
    )(*args)


def _ssd_kernel(xbc_ref, z_ref, small_ref, cw_ref, cb_ref, dtb_ref, alog_ref, dsk_ref, nw_ref,
                o_ref, hout_ref, cs_ref, ext_ref, h_ref, *, chunk):
    c = pl.program_id(1)
    nc = pl.num_programs(1)
    L = chunk
    pad = 8

    @pl.when(c == 0)
    def _():
        ext_ref[0:pad, :] = jnp.zeros((pad, CONV_DIM), F32)
        h_ref[...] = jnp.zeros_like(h_ref)

    xin = xbc_ref[...]
    ext_ref[pad:pad + L, :] = xin
    y = cb_ref[...] + cw_ref[CONV_W - 1:CONV_W, :] * xin
    for j in range(CONV_W - 1):
        sft = CONV_W - 1 - j
        y = y + cw_ref[j:j + 1, :] * ext_ref[pad - sft:pad - sft + L, :]
    tail = xin[L - pad:L, :]
    ext_ref[0:pad, :] = tail
    cs_ref[...] = tail
    xbc = _silu(y)
    xs = xbc[:, :S_WIDTH]

    dt_raw = small_ref[...] + dtb_ref[...]
    dt = jnp.maximum(dt_raw, 0.0) + jnp.log(1.0 + jnp.exp(-jnp.abs(dt_raw)))
    da = dt * (-jnp.exp(alog_ref[...]))
    r = lax.broadcasted_iota(I32, (L, L), 0)
    cc = lax.broadcasted_iota(I32, (L, L), 1)
    causal = cc <= r
    tri = jnp.where(causal, 1.0, 0.0).astype(F32)
    a_cs = jnp.dot(tri, da, preferred_element_type=F32, precision=lax.Precision.HIGHEST)
    a_cs_t = a_cs.T
    a_end = a_cs[L - 1:L, :]
    to_end = jnp.exp(a_end - a_cs)
    exp_cs = jnp.exp(a_cs)
    exp_end = jnp.exp(a_end)

    xds = [xs[:, hd * S_HEADDIM:(hd + 1) * S_HEADDIM] * dt[:, 8 + hd:9 + hd] for hd in range(S_HEADS)]
    xw_t = jnp.concatenate([xds[hd] * to_end[:, 8 + hd:9 + hd] for hd in range(S_HEADS)], axis=1).T
    ys = []
    for g in range(S_GROUPS):
        bm = xbc[:, S_WIDTH + g * D_STATE:S_WIDTH + (g + 1) * D_STATE].astype(BF16)
        cm = xbc[:, S_WIDTH + (S_GROUPS + g) * D_STATE:S_WIDTH + (S_GROUPS + g + 1) * D_STATE].astype(BF16)
        cb = _dot_nt(cm, bm)
        for rr in range(HPG):
            hd = g * HPG + rr
            col = 8 + hd
            x_h = xs[:, hd * S_HEADDIM:(hd + 1) * S_HEADDIM]
            xd = xds[hd]
            seg = a_cs[:, col:col + 1] - a_cs_t[col:col + 1, :]
            decay = jnp.where(causal, jnp.exp(jnp.where(causal, seg, 0.0)), 0.0)
            y_diag = _dot((cb * decay).astype(BF16), xd.astype(BF16))
            h_prev = h_ref[hd]
            y_off = _dot_nt(cm, h_prev.astype(BF16)) * exp_cs[:, col:col + 1]
            states = _dot(xw_t[hd * S_HEADDIM:(hd + 1) * S_HEADDIM, :].astype(BF16), bm)
            h_ref[hd] = h_prev * exp_end[:, col:col + 1] + states
            ys.append(y_diag + y_off + x_h * dsk_ref[:, col:col + 1])
    yy = jnp.concatenate(ys, axis=1) * _silu(z_ref[...])
    gw = S_WIDTH // S_GROUPS
    outs = []
    for g in range(S_GROUPS):
        yg = yy[:, g * gw:(g + 1) * gw]
        ms = jnp.mean(yg * yg, axis=-1, keepdims=True)
        outs.append(yg * lax.rsqrt(ms + EPS) * nw_ref[:, g * gw:(g + 1) * gw])
    o_ref[...] = jnp.concatenate(outs, axis=1).astype(o_ref.dtype)

    @pl.when(c == nc - 1)
    def _():
        hout_ref[...] = h_ref[...]


def _lane_pack(v, off=8):
    return jnp.zeros((1, SMALL_W), F32).at[0, off:off + v.shape[0]].set(v.astype(F32))


def _ssd_prompt(sxbc, sz, small, p, *, nb, seq, chunk):
    n = sxbc.shape[0]
    ncn = seq // chunk
    row = lambda w: pl.BlockSpec((chunk, w), lambda b, c: (b * ncn + c, 0))
    full = lambda a: pl.BlockSpec(a.shape, lambda b, c: (0,) * a.ndim)
    cw = p['ssm_conv_w']
    cb = p['ssm_conv_b'].reshape(1, CONV_DIM)
    dtb = _lane_pack(p['ssm_dt_bias'])
    alog = _lane_pack(p['ssm_a_log'])
    dsk = _lane_pack(p['ssm_d'])
    nw = p['ssm_norm'].reshape(1, S_WIDTH)
    o, hout, cs = pl.pallas_call(
        functools.partial(_ssd_kernel, chunk=chunk),
        grid=(nb, ncn),
        in_specs=[row(CONV_DIM), row(S_WIDTH), row(SMALL_W), full(cw), full(cb), full(dtb), full(alog),
                  full(dsk), full(nw)],
        out_specs=[row(S_WIDTH),
                   pl.BlockSpec((None, S_HEADS, S_HEADDIM, D_STATE), lambda b, c: (b, 0, 0, 0)),
                   pl.BlockSpec((None, 8, CONV_DIM), lambda b, c: (b, 0, 0))],
        out_shape=[jax.ShapeDtypeStruct((n, S_WIDTH), BF16),
                   jax.ShapeDtypeStruct((nb, S_HEADS, S_HEADDIM, D_STATE), F32),
                   jax.ShapeDtypeStruct((nb, 8, CONV_DIM), F32)],
        scratch_shapes=[pltpu.VMEM((8 + chunk, CONV_DIM), F32),
                        pltpu.VMEM((S_HEADS, S_HEADDIM, D_STATE), F32)],
        compiler_params=_cparams(("arbitrary", "arbitrary")),
        name="ssd",
    )(sxbc, sz, small, cw, cb, dtb, alog, dsk, nw)
    return o, hout, cs[:, 8 - (CONV_W - 1):, :]


def _ssd_step_kernel(xbc_ref, buf_ref, z_ref, small_ref, h_ref, cw_ref, cb_ref, dtb_ref, alog_ref, dsk_ref,
                     nw_ref, o_ref, hout_ref, cs_ref):
    xin = xbc_ref[...]
    buf = buf_ref[...]
    y = cb_ref[...] + cw_ref[CONV_W - 1:CONV_W, :] * xin
    for j in range(CONV_W - 1):
        y = y + cw_ref[j:j + 1, :] * buf[j:j + 1, :]
    cs_ref[...] = jnp.concatenate([buf[1:CONV_W - 1, :], xin], axis=0)
    xbc = _silu(y)
    xs = xbc[:, :S_WIDTH]
    dt_raw = small_ref[...] + dtb_ref[...]
    dt = jnp.maximum(dt_raw, 0.0) + jnp.log(1.0 + jnp.exp(-jnp.abs(dt_raw)))
    dec = jnp.exp(dt * (-jnp.exp(alog_ref[...])))
    rows = S_HEADS * S_HEADDIM
    er = lax.broadcasted_iota(I32, (SMALL_W, rows), 0)
    ec = lax.broadcasted_iota(I32, (SMALL_W, rows), 1)
    expand = jnp.where(er - 8 == ec // S_HEADDIM, 1.0, 0.0).astype(F32)
    stacked = jnp.concatenate([dt, dec, dsk_ref[...], jnp.zeros((5, SMALL_W), F32)], axis=0)
    per_pos = jnp.dot(stacked, expand, preferred_element_type=F32, precision=lax.Precision.HIGHEST)
    dt_row, dec_row, dsk_row = per_pos[0:1, :], per_pos[1:2, :], per_pos[2:3, :]
    xd_row = xs * dt_row

    def to_col(v):
        return jnp.broadcast_to(v, (LANES, rows)).T

    xd_col = to_col(xd_row)[:, :D_STATE]
    dec_col = to_col(dec_row)[:, :D_STATE]
    half = rows // S_GROUPS
    bsel = lax.broadcasted_iota(I32, (rows, D_STATE), 0) < half
    b0 = xbc[:, S_WIDTH:S_WIDTH + D_STATE]
    b1 = xbc[:, S_WIDTH + D_STATE:S_WIDTH + 2 * D_STATE]
    c0 = xbc[:, S_WIDTH + 2 * D_STATE:S_WIDTH + 3 * D_STATE]
    c1 = xbc[:, S_WIDTH + 3 * D_STATE:S_WIDTH + 4 * D_STATE]
    bmat = jnp.where(bsel, jnp.broadcast_to(b0, (rows, D_STATE)), jnp.broadcast_to(b1, (rows, D_STATE)))
    cmat = jnp.where(bsel, jnp.broadcast_to(c0, (rows, D_STATE)), jnp.broadcast_to(c1, (rows, D_STATE)))
    h = h_ref[...].reshape(rows, D_STATE)
    hn = h * dec_col + xd_col * bmat
    hout_ref[...] = hn.reshape(S_HEADS, S_HEADDIM, D_STATE)
    ycol = jnp.sum(hn.astype(BF16).astype(F32) * cmat.astype(BF16).astype(F32), axis=-1, keepdims=True)
    yrow = jnp.broadcast_to(ycol, (rows, LANES)).T[0:1, :]
    yy = (yrow + xs * dsk_row) * _silu(z_ref[...])
    gw = S_WIDTH // S_GROUPS
    outs = []
    for g in range(S_GROUPS):
        yg = yy[:, g * gw:(g + 1) * gw]
        ms = jnp.mean(yg * yg, axis=-1, keepdims=True)
        outs.append(yg * lax.rsqrt(ms + EPS) * nw_ref[:, g * gw:(g + 1) * gw])
    o_ref[...] = jnp.concatenate(outs, axis=1).astype(o_ref.dtype)


def _ssd_sample(sxbc, conv_buf, sz, small, h0, p):
    nb = sxbc.shape[0]
    cw = p['ssm_conv_w']
    cb = p['ssm_conv_b'].reshape(1, CONV_DIM)
    dtb = _lane_pack(p['ssm_dt_bias'])
    alog = _lane_pack(p['ssm_a_log'])
    dsk = _lane_pack(p['ssm_d'])
    nw = p['ssm_norm'].reshape(1, S_WIDTH)
    full = lambda a: pl.BlockSpec(a.shape, lambda b: (0,) * a.ndim)
    r3 = lambda w: pl.BlockSpec((None, 1, w), lambda b: (b, 0, 0))
    o, hout, cs = pl.pallas_call(
        _ssd_step_kernel,
        grid=(nb,),
        in_specs=[r3(CONV_DIM), pl.BlockSpec((None, CONV_W - 1, CONV_DIM), lambda b: (b, 0, 0)),
                  r3(S_WIDTH), r3(SMALL_W),
                  pl.BlockSpec((None, S_HEADS, S_HEADDIM, D_STATE), lambda b: (b, 0, 0, 0)),
                  full(cw), full(cb), full(dtb), full(alog), full(dsk), full(nw)],
        out_specs=[r3(S_WIDTH),
                   pl.BlockSpec((None, S_HEADS, S_HEADDIM, D_STATE), lambda b: (b, 0, 0, 0)),
                   pl.BlockSpec((None, CONV_W - 1, CONV_DIM), lambda b: (b, 0, 0))],
        out_shape=[jax.ShapeDtypeStruct((nb, 1, S_WIDTH), BF16),
                   jax.ShapeDtypeStruct((nb, S_HEADS, S_HEADDIM, D_STATE), F32),
                   jax.ShapeDtypeStruct((nb, CONV_W - 1, CONV_DIM), F32)],
        compiler_params=_cparams(("arbitrary",)),
        name="ssd_step",
    )(sxbc.reshape(nb, 1, CONV_DIM), conv_buf, sz.reshape(nb, 1, S_WIDTH), small.reshape(nb, 1, SMALL_W),
      h0, cw, cb, dtb, alog, dsk, nw)
    return o.reshape(nb, S_WIDTH), hout, cs


def _decode_kernel(pt_ref, qa_ref, kan_ref, van_ref, qf_ref, kfn_ref, vfn_ref, lfn_ref, lp_ref, sub_ref,
                   *refs, lam_init, pages_per_step, n_steps):
    P = pages_per_step
    kd_refs = refs[0:P]
    kf_refs = refs[P:2 * P]
    lf_refs = refs[2 * P:3 * P]
    vd_refs = refs[3 * P:4 * P]
    vf_refs = refs[4 * P:5 * P]
    oa_ref, of_ref = refs[5 * P:5 * P + 2]
    (qd_ref, qfm_ref, sd_ref, sf_ref, md_ref, mf_ref, ld_ref, lfs_ref, accd_ref, accf_ref, suf_ref,
     ssd_ref, ssf_ref) = refs[5 * P + 2:]
    j = pl.program_id(1)
    W = A_WIDTH
    n_pages = n_steps * P
    bf = lambda a: a.astype(BF16).astype(F32)

    @pl.when(j == 0)
    def _():
        seg16 = lax.broadcasted_iota(I32, (16, W), 1) // HEAD_DIM
        row16 = lax.broadcasted_iota(I32, (16, W), 0)
        want = jnp.where(row16 < 8, 2 * row16, 2 * (row16 - 8) + 1)
        keep = jnp.logical_and(seg16 == want, (row16 % 8) < A_HEADS)
        qd_ref[...] = jnp.where(keep, jnp.broadcast_to(qa_ref[...].astype(F32), (16, W)), 0.0).astype(BF16)
        seg8 = lax.broadcasted_iota(I32, (8, W), 1) // HEAD_DIM
        row8 = lax.broadcasted_iota(I32, (8, W), 0)
        qfm_ref[...] = jnp.where(seg8 == row8, jnp.broadcast_to(qf_ref[...].astype(F32), (8, W)), 0.0).astype(BF16)
        md_ref[...] = jnp.full(md_ref.shape, -jnp.inf, F32)
        mf_ref[...] = jnp.full(mf_ref.shape, -jnp.inf, F32)
        accd_ref[...] = jnp.zeros_like(accd_ref)
        accf_ref[...] = jnp.zeros_like(accf_ref)
        suf_ref[...] = jnp.zeros_like(suf_ref)

    lane = lax.broadcasted_iota(I32, (8, PAGE), 1)

    @pl.when(j < n_steps)
    def _():
        for pp in range(P):
            g = j * P + pp
            sd = _dot(qd_ref[...], kd_refs[pp][...].astype(BF16))
            sd_ref[g] = sd
            md_ref[...] = jnp.maximum(md_ref[...], jnp.max(sd, axis=-1, keepdims=True))
            lf = lf_refs[pp][...]
            inc = lf
            d = 1
            while d < PAGE:
                inc = inc + jnp.where(lane + d < PAGE, pltpu.roll(inc, PAGE - d, axis=1), 0.0)
                d *= 2
            sfx = inc - lf + suf_ref[...]
            sf = (_dot(qfm_ref[...], kf_refs[pp][...].astype(BF16)) + lfn_ref[...]) + sfx
            sf_ref[g] = sf
            mf_ref[...] = jnp.maximum(mf_ref[...], jnp.max(sf, axis=-1, keepdims=True))
            suf_ref[...] = suf_ref[...] + jnp.sum(lf, axis=-1, keepdims=True)

    @pl.when(j == n_steps)
    def _():
        s_self_d = jnp.sum(qd_ref[...].astype(F32) * bf(kan_ref[...]), axis=-1, keepdims=True)
        dnew = lfn_ref[...]
        s_self_f = (jnp.sum(qfm_ref[...].astype(F32) * bf(kfn_ref[...]), axis=-1, keepdims=True) + dnew) - dnew
        m_d = jnp.maximum(md_ref[...], s_self_d)
        m_f = jnp.maximum(mf_ref[...], s_self_f)
        e_self_d = jnp.exp(s_self_d - m_d)
        e_self_f = jnp.exp(s_self_f - m_f)

        def body(g, carry):
            l_d, l_f = carry
            e_d = jnp.exp(sd_ref[g] - m_d)
            sd_ref[g] = e_d
            e_f = jnp.exp(sf_ref[g] - m_f)
            sf_ref[g] = e_f
            return (l_d + jnp.sum(e_d, axis=-1, keepdims=True), l_f + jnp.sum(e_f, axis=-1, keepdims=True))

        l_d, l_f = lax.fori_loop(0, n_pages, body, (jnp.zeros((16, 1), F32), jnp.zeros((8, 1), F32)))
        ld_ref[...] = l_d + e_self_d
        lfs_ref[...] = l_f + e_self_f
        ssd_ref[...] = e_self_d
        ssf_ref[...] = e_self_f

    @pl.when(j >= n_steps)
    def _():
        lam = _lam(lp_ref) + lam_init
        l_d = ld_ref[...]
        l_f = lfs_ref[...]
        for pp in range(P):
            g = (j - n_steps) * P + pp
            e_d = sd_ref[g]
            a = e_d[0:8] / l_d[0:8] - lam * (e_d[8:16] / l_d[8:16])
            ab = a.astype(BF16)
            vd = vd_refs[pp]
            pv = jnp.concatenate([_dot(ab, vd[:, hh, :].astype(BF16)) for hh in range(A_HEADS)], axis=1)
            accd_ref[...] = accd_ref[...] + pv
            accf_ref[...] = accf_ref[...] + _dot_nt((sf_ref[g] / l_f).astype(BF16), vf_refs[pp][...].astype(BF16))

    @pl.when(j == 2 * n_steps - 1)
    def _():
        lam = _lam(lp_ref) + lam_init
        pr = ssd_ref[...] / ld_ref[...]
        a_self = bf(pr[0:8] - lam * pr[8:16])
        od = accd_ref[...] + a_self * bf(van_ref[...])
        hsel = lax.broadcasted_iota(I32, (8, W), 1) // (2 * HEAD_DIM)
        rowi = lax.broadcasted_iota(I32, (8, W), 0)
        o = jnp.sum(jnp.where(rowi == hsel, od, 0.0), axis=0, keepdims=True)
        outs = []
        for hh in range(A_HEADS):
            og = o[:, hh * 2 * HEAD_DIM:(hh + 1) * 2 * HEAD_DIM]
            ms = jnp.mean(og * og, axis=-1, keepdims=True)
            outs.append(og * lax.rsqrt(ms + EPS) * sub_ref[...] * (1.0 - lam_init))
        oa_ref[...] = jnp.concatenate(outs, axis=1).astype(oa_ref.dtype)
        p_self = bf(ssf_ref[...] / lfs_ref[...])
        of = accf_ref[...] + p_self * bf(vfn_ref[...])
        seg8 = lax.broadcasted_iota(I32, (8, W), 1) // HEAD_DIM
        of_ref[...] = jnp.sum(jnp.where(seg8 == rowi, of, 0.0), axis=0, keepdims=True).astype(of_ref.dtype)


PAGE = 128


def _decode(layer, page_table, qa, ka_new, va_new, qf, kf_new, vf_new, lf_new_col, lam_params, subln,
            ck_a, cv_a, ck_f, cv_f, clf_t, lam_init, *, pages_per_step):
    nb, n_pages = page_table.shape
    P = pages_per_step
    n_steps = n_pages // P
    W = A_WIDTH

    def page_spec(pp, shape_tail, value_phase):
        def imap(b, j, pt):
            step = jnp.maximum(j - n_steps, 0) if value_phase else jnp.minimum(j, n_steps - 1)
            return (layer, pt[b, n_pages - 1 - (step * P + pp)]) + (0,) * len(shape_tail)
        return pl.BlockSpec((None, None) + shape_tail, imap)

    rowspec = pl.BlockSpec((None, 1, W), lambda b, j, pt: (b, 0, 0))
    in_specs = [rowspec, rowspec, rowspec, rowspec, rowspec, rowspec,
                pl.BlockSpec((None, 8, 1), lambda b, j, pt: (b, 0, 0)),
                pl.BlockSpec((4, HEAD_DIM), lambda b, j, pt: (0, 0)),
                pl.BlockSpec((1, LANES), lambda b, j, pt: (0, 0))]
    args = [qa, ka_new, va_new, qf, kf_new, vf_new, lf_new_col, lam_params, subln.reshape(1, LANES)]
    for arr, tail, value_phase in ((ck_a, (W, PAGE), False), (ck_f, (W, PAGE), False),
                                   (clf_t, (F_HEADS, PAGE), False), (cv_a, (PAGE, A_HEADS, 2 * HEAD_DIM), True),
                                   (cv_f, (W, PAGE), True)):
        for pp in range(P):
            in_specs.append(page_spec(pp, tail, value_phase))
            args.append(arr)
    grid_spec = pltpu.PrefetchScalarGridSpec(
        num_scalar_prefetch=1,
        grid=(nb, 2 * n_steps),
        in_specs=in_specs,
        out_specs=[rowspec, rowspec],
        scratch_shapes=[pltpu.VMEM((16, W), BF16), pltpu.VMEM((8, W), BF16),
                        pltpu.VMEM((n_pages, 16, PAGE), F32), pltpu.VMEM((n_pages, 8, PAGE), F32),
                        pltpu.VMEM((16, 1), F32), pltpu.VMEM((8, 1), F32),
                        pltpu.VMEM((16, 1), F32), pltpu.VMEM((8, 1), F32),
                        pltpu.VMEM((8, W), F32), pltpu.VMEM((8, W), F32), pltpu.VMEM((8, 1), F32),
                        pltpu.VMEM((16, 1), F32), pltpu.VMEM((8, 1), F32)],
    )
    oa, of = pl.pallas_call(
        functools.partial(_decode_kernel, lam_init=lam_init, pages_per_step=P, n_steps=n_steps),
        grid_spec=grid_spec,
        out_shape=[jax.ShapeDtypeStruct((nb, 1, W), BF16), jax.ShapeDtypeStruct((nb, 1, W), BF16)],
        compiler_params=_cparams(("arbitrary", "arbitrary")),
        name="decode_attn",
    )(page_table, *args)
    return oa.reshape(nb, W), of.reshape(nb, W)


def _merge_kernel(x_ref, sh_ref, sc_ref, g1_ref, nw_ref, oa_ref, of_ref, os_ref, wg_ref, wa_ref, wf_ref, ws_ref,
                  wo_ref, o_ref):
    x = x_ref[...]
    d = x.shape[1]
    h = _norm_mod(x, nw_ref[...], sc_ref[...], sh_ref[...]).astype(BF16)
    m = None
    for j, (b_ref, w_ref) in enumerate(((oa_ref, wa_ref), (of_ref, wf_ref), (os_ref, ws_ref))):
        g = _sigmoid(_dot(h, wg_ref[:, j * d:(j + 1) * d]))
        t = g * _dot(b_ref[...], w_ref[...])
        m = t if m is None else m + t
    out = _dot(m.astype(BF16), wo_ref[...])
    o_ref[...] = x + g1_ref[...] * out


def _merge(x, shift, scale, gate, nw, oa, of, os_, wg, wa, wf, ws, wo, *, tm, tiles_per_batch, per_row_mod):
    n, d = x.shape
    if per_row_mod:
        mod_spec = pl.BlockSpec((tm, d), lambda i: (i, 0))
    else:
        mod_spec = pl.BlockSpec((None, 1, d), lambda i: (i // tiles_per_batch, 0, 0))
    row = lambda w: pl.BlockSpec((tm, w), lambda i: (i, 0))
    full = lambda a: pl.BlockSpec(a.shape, lambda i: (0,) * a.ndim)
    return pl.pallas_call(
        _merge_kernel,
        grid=(n // tm,),
        in_specs=[row(d), mod_spec, mod_spec, mod_spec, pl.BlockSpec((1, d), lambda i: (0, 0)),
                  row(A_WIDTH), row(F_WIDTH), row(S_WIDTH), full(wg), full(wa), full(wf), full(ws), full(wo)],
        out_specs=row(d),
        out_shape=jax.ShapeDtypeStruct((n, d), F32),
        compiler_params=_cparams(("arbitrary",)),
        name="merge",
    )(x, shift, scale, gate, nw, oa, of, os_, wg, wa, wf, ws, wo)


def _route_kernel(x_ref, sh_ref, sc_ref, nw_ref, rw_ref, rb_ref, h_ref, idx_ref, wt_ref):
    h = _norm_mod(x_ref[...], nw_ref[...], sc_ref[...], sh_ref[...])
    h_ref[...] = h
    tm = h.shape[0]
    logits = _dot_nt(rw_ref[...].astype(BF16), h.astype(BF16))
    scores = _sigmoid(logits)
    sel = scores + rb_ref[...]
    per = N_EXPERTS // N_EXPERT_GROUPS
    ei = lax.broadcasted_iota(I32, (N_EXPERTS, tm), 0).astype(F32)
    gid = jnp.floor(ei * (1.0 / per))
    BIG = 1e9
    gs_rows = []
    li = lax.broadcasted_iota(I32, (per, tm), 0).astype(F32)
    for g in range(N_EXPERT_GROUPS):
        sg = sel[g * per:(g + 1) * per, :]
        m1 = jnp.max(sg, axis=0, keepdims=True)
        i1 = jnp.min(jnp.where(sg == m1, li, BIG), axis=0, keepdims=True)
        m2 = jnp.max(jnp.where(li == i1, -jnp.inf, sg), axis=0, keepdims=True)
        gs_rows.append(m1 + m2)
    gs = jnp.concatenate(gs_rows, axis=0)
    gi = lax.broadcasted_iota(I32, (N_EXPERT_GROUPS, tm), 0).astype(F32)
    cur = jnp.full((N_EXPERTS, tm), NEG, F32)
    for _ in range(TOPK_GROUPS):
        m = jnp.max(gs, axis=0, keepdims=True)
        pick = jnp.min(jnp.where(gs == m, gi, BIG), axis=0, keepdims=True)
        gs = jnp.where(gi == pick, -jnp.inf, gs)
        cur = jnp.where(gid == pick, sel, cur)
    idx_rows, w_rows = [], []
    for _ in range(TOP_K):
        m = jnp.max(cur, axis=0, keepdims=True)
        pick = jnp.min(jnp.where(cur == m, ei, BIG), axis=0, keepdims=True)
        hit = ei == pick
        w_rows.append(jnp.sum(jnp.where(hit, scores, 0.0), axis=0, keepdims=True))
        idx_rows.append(pick)
        cur = jnp.where(hit, -jnp.inf, cur)
    w = jnp.concatenate(w_rows, axis=0)
    idx_ref[...] = jnp.concatenate(idx_rows, axis=0).astype(I32)
    wt_ref[...] = w / jnp.sum(w, axis=0, keepdims=True) * ROUTED_SCALE


def _route(x, shift, scale, nw, rw_t, rb_col, *, tm, tiles_per_batch, per_row_mod):
    n, d = x.shape
    if per_row_mod:
        mod_spec = pl.BlockSpec((tm, d), lambda i: (i, 0))
    else:
        mod_spec = pl.BlockSpec((None, 1, d), lambda i: (i // tiles_per_batch, 0, 0))
    row = lambda w: pl.BlockSpec((tm, w), lambda i: (i, 0))
    colspec = pl.BlockSpec((TOP_K, tm), lambda i: (0, i))
    return pl.pallas_call(
        _route_kernel,
        grid=(n // tm,),
        in_specs=[row(d), mod_spec, mod_spec, pl.BlockSpec((1, d), lambda i: (0, 0)),
                  pl.BlockSpec((N_EXPERTS, d), lambda i: (0, 0)), pl.BlockSpec((N_EXPERTS, 1), lambda i: (0, 0))],
        out_specs=[row(d), colspec, colspec],
        out_shape=[jax.ShapeDtypeStruct((n, d), F32), jax.ShapeDtypeStruct((TOP_K, n), I32),
                   jax.ShapeDtypeStruct((TOP_K, n), F32)],
        compiler_params=_cparams(("arbitrary",)),
        name="route",
    )(x, shift, scale, nw, rw_t, rb_col)


def _expert_kernel(be_ref, nu_ref, tok_hbm, x_hbm, wg_ref, wu_ref, wd_ref, y_ref, tok_smem, xbuf, isem, gsem, *, blk):
    i = pl.program_id(0)
    n_used = nu_ref[0]
    slot = i % 2

    def idx_copy(b, s):
        return pltpu.make_async_copy(tok_hbm.at[b], tok_smem.at[s], isem.at[s])

    @pl.when(i == 0)
    def _():
        idx_copy(0, 0).start()

    @pl.when(i < n_used)
    def _():
        idx_copy(i, slot).wait()

        @pl.when(i + 1 < n_used)
        def _():
            idx_copy(i + 1, 1 - slot).start()

        def issue(r, carry):
            tok = tok_smem[slot, r]
            pltpu.make_async_copy(x_hbm.at[pl.ds(tok, 1), :], xbuf.at[slot, pl.ds(r, 1), :], gsem.at[slot]).start()
            return carry

        lax.fori_loop(0, blk, issue, 0, unroll=8)

    @pl.when(jnp.logical_and(i >= 1, i - 1 < n_used))
    def _():
        ps = 1 - slot
        pltpu.make_async_copy(x_hbm.at[pl.ds(0, blk), :], xbuf.at[ps], gsem.at[ps]).wait()
        xb = xbuf[ps].astype(BF16)
        hg = _dot(xb, wg_ref[...])
        hu = _dot(xb, wu_ref[...])
        hh = (_silu(hg) * hu).astype(BF16)
        y_ref[...] = _dot(hh, wd_ref[...])

    @pl.when(jnp.logical_and(i >= 1, i - 1 >= n_used))
    def _():
        y_ref[...] = jnp.zeros_like(y_ref)


def _experts(x, slot_tok, block_e, n_used, wg, wu, wd, *, blk):
    n, d = x.shape
    n_blocks = slot_tok.shape[0]
    de = wg.shape[2]

    def wmap(i, be, nu):
        return (be[jnp.clip(i - 1, 0, n_blocks - 1)], 0, 0)

    def ymap(i, be, nu):
        return (jnp.maximum(i - 1, 0), 0)

    grid_spec = pltpu.PrefetchScalarGridSpec(
        num_scalar_prefetch=2,
        grid=(n_blocks + 1,),
        in_specs=[pl.BlockSpec(memory_space=pl.ANY), pl.BlockSpec(memory_space=pl.ANY),
                  pl.BlockSpec((None, d, de), wmap), pl.BlockSpec((None, d, de), wmap),
                  pl.BlockSpec((None, de, d), wmap)],
        out_specs=pl.BlockSpec((blk, d), ymap),
        scratch_shapes=[pltpu.SMEM((2, blk), I32), pltpu.VMEM((2, blk, d), F32),
                        pltpu.SemaphoreType.DMA((2,)), pltpu.SemaphoreType.DMA((2,))],
    )
    return pl.pallas_call(
        functools.partial(_expert_kernel, blk=blk),
        grid_spec=grid_spec,
        out_shape=jax.ShapeDtypeStruct((n_blocks * blk, d), F32),
        compiler_params=_cparams(("arbitrary",)),
        name="experts",
    )(block_e, n_used, slot_tok, x, wg, wu, wd)


def _combine_kernel(pos_hbm, y_hbm, x_ref, h_ref, g2_ref, wt_ref, sg_ref, su_ref, sd_ref, fin_ref, o_ref,
                    pos_smem, ybuf, isem, gsem, *, tc, n_tiles, final):
    i = pl.program_id(0)
    slot = i % 2

    def idx_copy(t, s):
        return pltpu.make_async_copy(pos_hbm.at[t], pos_smem.at[s], isem.at[s])

    @pl.when(i == 0)
    def _():
        idx_copy(0, 0).start()

    @pl.when(i < n_tiles)
    def _():
        idx_copy(i, slot).wait()

        @pl.when(i + 1 < n_tiles)
        def _():
            idx_copy(i + 1, 1 - slot).start()

        def issue(t, carry):
            for k in range(TOP_K):
                p = pos_smem[slot, t * TOP_K + k]
                pltpu.make_async_copy(y_hbm.at[pl.ds(p, 1), :], ybuf.at[slot, k, pl.ds(t, 1), :],
                                      gsem.at[slot]).start()
            return carry

        lax.fori_loop(0, tc, issue, 0, unroll=2)

    @pl.when(i >= 1)
    def _():
        ps = 1 - slot
        hb = h_ref[...].astype(BF16)
        sh = (_silu(_dot(hb, sg_ref[...])) * _dot(hb, su_ref[...])).astype(BF16)
        acc = _dot(sh, sd_ref[...])
        for k in range(TOP_K):
            pltpu.make_async_copy(y_hbm.at[pl.ds(0, tc), :], ybuf.at[ps, k], gsem.at[ps]).wait()
        w = wt_ref[...]
        for k in range(TOP_K):
            acc = acc + w[:, k:k + 1] * ybuf[ps, k]
        xo = x_ref[...] + g2_ref[...] * acc
        if final:
            ms = jnp.mean(xo * xo, axis=-1, keepdims=True)
            xo = xo * lax.rsqrt(ms + EPS) * fin_ref[...]
        o_ref[...] = xo


def _combine(pos_tiles, y_sorted, x, h, gate, wts, sg, su, sd, fin_w, *, tc, tiles_per_batch, per_row_mod, final):
    n, d = x.shape
    n_tiles = n // tc
    prev = lambda i: jnp.maximum(i - 1, 0)
    if per_row_mod:
        mod_spec = pl.BlockSpec((tc, d), lambda i: (prev(i), 0))
    else:
        mod_spec = pl.BlockSpec((None, 1, d), lambda i: (prev(i) // tiles_per_batch, 0, 0))
    row = lambda w: pl.BlockSpec((tc, w), lambda i: (prev(i), 0))
    full = lambda a: pl.BlockSpec(a.shape, lambda i: (0,) * a.ndim)
    return pl.pallas_call(
        functools.partial(_combine_kernel, tc=tc, n_tiles=n_tiles, final=final),
        grid=(n_tiles + 1,),
        in_specs=[pl.BlockSpec(memory_space=pl.ANY), pl.BlockSpec(memory_space=pl.ANY),
                  row(d), row(d), mod_spec, row(TOP_K), full(sg), full(su), full(sd),
                  pl.BlockSpec((1, d), lambda i: (0, 0))],
        out_specs=row(d),
        out_shape=jax.ShapeDtypeStruct((n, d), F32),
        scratch_shapes=[pltpu.SMEM((2, tc * TOP_K), I32), pltpu.VMEM((2, TOP_K, tc, d), F32),
                        pltpu.SemaphoreType.DMA((2,)), pltpu.SemaphoreType.DMA((2,))],
        compiler_params=_cparams(("arbitrary",)),
        name="combine",
    )(pos_tiles, y_sorted, x, h, gate, wts, sg, su, sd, fin_w)


def _dispatch_indices(idx, blk):
    n_tok = idx.shape[0]
    n_rows = n_tok * TOP_K
    n_blocks = -(-n_rows // blk) + N_EXPERTS
    flat_e = idx.reshape(-1)
    order = jnp.argsort(flat_e)
    sorted_e = flat_e[order]
    counts = jnp.bincount(flat_e, length=N_EXPERTS)
    padded = (counts + blk - 1) // blk * blk
    start = jnp.cumsum(counts) - counts
    pad_end = jnp.cumsum(padded)
    pad_start = pad_end - padded
    dest = (pad_start[sorted_e] + jnp.arange(n_rows) - start[sorted_e]).astype(I32)
    slot_tok = jnp.zeros((n_blocks * blk,), I32).at[dest].set((order // TOP_K).astype(I32))
    pos = jnp.zeros((n_rows,), I32).at[order].set(dest)
    block_e = jnp.minimum(jnp.searchsorted(pad_end, jnp.arange(n_blocks) * blk, side='right'),
                          N_EXPERTS - 1).astype(I32)
    n_used = (pad_end[-1] // blk).astype(I32).reshape(1)
    return slot_tok.reshape(n_blocks, blk), pos, block_e, n_used


def _moe(x1, mod, nw, lw, fin_w, *, tm, tc, blk, tiles_per_batch_tm, tiles_per_batch_tc, per_row_mod, final):
    shift2, scale2, gate2 = mod
    h2, idx_t, wts_t = _route(x1, shift2, scale2, nw, lw['router_w_t'], lw['router_b'], tm=tm,
                              tiles_per_batch=tiles_per_batch_tm, per_row_mod=per_row_mod)
    idx = idx_t.T
    wts = wts_t.T
    slot_tok, pos, block_e, n_used = _dispatch_indices(idx, blk)
    y_sorted = _experts(h2, slot_tok, block_e, n_used, lw['exp_g'], lw['exp_u'], lw['exp_d'], blk=blk)
    n = x1.shape[0]
    pos_tiles = pos.reshape(n // tc, tc * TOP_K)
    return _combine(pos_tiles, y_sorted, x1, h2, gate2, wts, lw['sh_g'], lw['sh_u'], lw['sh_d'], fin_w,
                    tc=tc, tiles_per_batch=tiles_per_batch_tc, per_row_mod=per_row_mod, final=final)


def _pack_w_in(w_in_l):
    d = w_in_l.shape[0]
    cuts = [0]
    for wdt in (A_WIDTH, A_WIDTH, A_WIDTH, F_WIDTH, F_WIDTH, F_WIDTH, F_HEADS, S_WIDTH, CONV_DIM, S_HEADS,
                N_BRANCH * d):
        cuts.append(cuts[-1] + wdt)
    seg = lambda k: w_in_l[:, cuts[k]:cuts[k + 1]]
    small = jnp.concatenate([seg(6), seg(9), jnp.zeros((d, SMALL_W - F_HEADS - S_HEADS), w_in_l.dtype)], axis=1)
    pack = jnp.concatenate([seg(0), seg(1), seg(2), seg(3), seg(4), seg(5), seg(7), seg(8), small], axis=1)
    w_row = jnp.concatenate([seg(1), seg(2), seg(4), seg(7), seg(8), small], axis=1)
    w_t = jnp.concatenate([seg(k) for k in range(6)], axis=1).T
    return pack.astype(BF16), seg(10).astype(BF16), w_row.astype(BF16), w_t.astype(BF16)


def _rope_tables(pos):
    half = ROT_DIM // 2
    inv = ROPE_THETA ** (-jnp.arange(half, dtype=F32) / half)
    ang = pos.astype(F32)[:, None] * inv[None, :]
    cos, sin = jnp.cos(ang), jnp.sin(ang)
    n = pos.shape[0]
    one = jnp.ones((n, HEAD_DIM - ROT_DIM), F32)
    zero_h = jnp.zeros((n, half), F32)
    zero_r = jnp.zeros((n, HEAD_DIM - ROT_DIM), F32)
    rc = jnp.concatenate([cos, cos, one], axis=1)
    rn = jnp.concatenate([-sin, zero_h, zero_r], axis=1)
    rp = jnp.concatenate([zero_h, sin, zero_r], axis=1)
    rep = LANES // HEAD_DIM
    return tuple(jnp.tile(t, (1, rep)) for t in (rc, rn, rp))


def kernel(x_prompt, x_sample, cache_diff_k, cache_diff_v, cache_fox_k, cache_fox_v, cache_fox_logf, state_ssm, state_conv, page_table, c_prompt, c_sample, w_ada, b_ada, norm_mix, norm_ffn, w_in, diff_lambda, diff_subln, fox_b_f, ssm_conv_w, ssm_conv_b, ssm_dt_bias, ssm_a_log, ssm_d, ssm_norm, w_branch_a, w_branch_f, w_branch_s, w_out, router_w, router_bias, exp_w_gate, exp_w_up, exp_w_down, shared_w_gate, shared_w_up, shared_w_down, norm_final):
    nbp, seq, d = x_prompt.shape
    nbs, tdec, _ = x_sample.shape
    depth = w_in.shape[0]
    n_phys, page = cache_diff_k.shape[1], cache_diff_k.shape[2]
    n_pages = page_table.shape[1]
    assert tdec == 1 and page == PAGE
    n_p = nbp * seq
    n_s = nbs

    tm = min(256, seq)
    tq = min(256, seq)
    chunk = min(128, seq)
    tc_p = min(128, seq)
    blk_p = 256
    blk_s = 16
    pps = next(c for c in (4, 2, 1) if n_pages % c == 0)

    xp = x_prompt.reshape(n_p, d)
    xs = x_sample.reshape(n_s, d)
    ck_a = jnp.transpose(cache_diff_k, (0, 1, 3, 4, 5, 2)).reshape(depth, n_phys, A_WIDTH, page)
    ck_f = jnp.transpose(cache_fox_k, (0, 1, 3, 4, 2)).reshape(depth, n_phys, F_WIDTH, page)
    cv_f = jnp.transpose(cache_fox_v, (0, 1, 3, 4, 2)).reshape(depth, n_phys, F_WIDTH, page)
    cv_a = cache_diff_v
    clf_t = jnp.swapaxes(cache_fox_logf, 2, 3)

    tabs_p = _rope_tables(jnp.arange(seq))
    tabs_p_t = tuple(t.T for t in tabs_p)
    ck_sel = _ck_select()
    tabs_s = _rope_tables(jnp.full((n_s,), n_pages * page, I32))

    n_c = nbp + nbs
    n_c_pad = -(-n_c // 8) * 8
    c_all = jnp.concatenate([c_prompt, c_sample, jnp.zeros((n_c_pad - n_c, d), F32)], axis=0)

    st_p, st_s = [], []
    y_p = y_s = None
    fin_w = norm_final.reshape(1, d)
    for l in range(depth):
        lam_init = 0.8 - 0.6 * math.exp(-0.3 * l)
        w_pack, w_gates, w_row, w_t = _pack_w_in(w_in[l])
        bias_small = jnp.zeros((1, SMALL_W), F32).at[0, :F_HEADS].set(fox_b_f[l])
        lw = {'router_w_t': router_w[l].T, 'router_b': router_bias[l].reshape(N_EXPERTS, 1),
              'exp_g': exp_w_gate[l].astype(BF16), 'exp_u': exp_w_up[l].astype(BF16),
              'exp_d': exp_w_down[l].astype(BF16), 'sh_g': shared_w_gate[l].astype(BF16),
              'sh_u': shared_w_up[l].astype(BF16), 'sh_d': shared_w_down[l].astype(BF16)}
        wa, wf, ws = (w_branch_a[l].astype(BF16), w_branch_f[l].astype(BF16), w_branch_s[l].astype(BF16))
        wo = w_out[l].astype(BF16)
        sp = {'ssm_conv_w': ssm_conv_w[l], 'ssm_conv_b': ssm_conv_b[l], 'ssm_dt_bias': ssm_dt_bias[l],
              'ssm_a_log': ssm_a_log[l], 'ssm_d': ssm_d[l], 'ssm_norm': ssm_norm[l]}
        nm = norm_mix[l].reshape(1, d)
        nf = norm_ffn[l].reshape(1, d)
        final = l == depth - 1

        mod = _ada(c_all, w_ada[l], b_ada[l])
        mod6 = [mod[:, k * d:(k + 1) * d] for k in range(6)]
        mod_p = [m[:nbp].reshape(nbp, 1, d) for m in mod6]
        mod_s = [m[nbp:nbp + nbs] for m in mod6]

        (qat, kab, kat, va, vat, qft, kfb, ckf, kft, vft32, vft, sz, sxbc, small, lf, cum) = _proj_t(
            xp, mod_p[0], mod_p[1], nm, w_row, w_t, bias_small, tabs_p, tabs_p_t, ck_sel, nb=nbp, seq=seq, tm=tm)
        o_a = _attn(qat, kab, None, vat, diff_lambda[l], diff_subln[l], lam_init, fox=False, nb=nbp, seq=seq,
                    tq=tq)
        o_f = _attn(qft, kfb, ckf, vft, None, None, 0.0, fox=True, nb=nbp, seq=seq, tq=tq)
        o_s, h_fin, conv_p = _ssd_prompt(sxbc, sz, small, sp, nb=nbp, seq=seq, chunk=chunk)
        x1 = _merge(xp, mod_p[0], mod_p[1], mod_p[2], nm, o_a, o_f, o_s, w_gates, wa, wf, ws, wo, tm=tm,
                    tiles_per_batch=seq // tm, per_row_mod=False)
        xp = _moe(x1, mod_p[3:6], nf, lw, fin_w, tm=tm, tc=tc_p, blk=blk_p, tiles_per_batch_tm=seq // tm,
                  tiles_per_batch_tc=seq // tc_p, per_row_mod=False, final=final)
        st_p.append((kat, va.reshape(nbp, seq, A_HEADS, 2 * HEAD_DIM), kft, vft32,
                     lf[:, :F_HEADS].reshape(nbp, seq, F_HEADS), h_fin, conv_p))

        (qa, ka, kab, va, vab, qf, kf, kfb, vf, vfb, sz, sxbc, small, lf, cum) = _proj(
            xs, mod_s[0], mod_s[1], nm, w_pack, bias_small, tabs_s, tm=n_s, tiles_per_batch=1, per_row_mod=True)
        lf_new_col = lf[:, :F_HEADS].reshape(n_s, F_HEADS, 1)
        r3 = lambda a: a.reshape(n_s, 1, a.shape[1])
        o_a, o_f = _decode(l, page_table, r3(qa), r3(ka), r3(va), r3(qf), r3(kf), r3(vf), lf_new_col,
                           diff_lambda[l], diff_subln[l], ck_a, cv_a, ck_f, cv_f, clf_t, lam_init,
                           pages_per_step=pps)
        o_s, h_new, conv_s = _ssd_sample(sxbc, state_conv[l], sz, small, state_ssm[l], sp)
        x1 = _merge(xs, mod_s[0], mod_s[1], mod_s[2], nm, o_a, o_f, o_s, w_gates, wa, wf, ws, wo, tm=n_s,
                    tiles_per_batch=1, per_row_mod=True)
        xs = _moe(x1, mod_s[3:6], nf, lw, fin_w, tm=n_s, tc=n_s, blk=blk_s, tiles_per_batch_tm=1,
                  tiles_per_batch_tc=1, per_row_mod=True, final=final)
        st_s.append((ka.reshape(n_s, 1, A_HEADS, 2, HEAD_DIM), va.reshape(n_s, 1, A_HEADS, 2 * HEAD_DIM),
                     kf.reshape(n_s, 1, F_HEADS, HEAD_DIM), vf.reshape(n_s, 1, F_HEADS, HEAD_DIM),
                     lf[:, :F_HEADS].reshape(n_s, 1, F_HEADS), h_new, conv_s))

    y_p = xp.reshape(nbp, seq, d)
    y_s = xs.reshape(nbs, 1, d)
    stk = lambda states, i: jnp.stack([s[i] for s in states])

    def from_feature_major(a, inner):
        a = a.reshape((depth, nbp) + inner + (seq,))
        return jnp.moveaxis(a, -1, 2)

    return (y_p, y_s,
            from_feature_major(stk(st_p, 0), (A_HEADS, 2, HEAD_DIM)), stk(st_s, 0), stk(st_p, 1), stk(st_s, 1),
            from_feature_major(stk(st_p, 2), (F_HEADS, HEAD_DIM)), stk(st_s, 2),
            from_feature_major(stk(st_p, 3), (F_HEADS, HEAD_DIM)), stk(st_s, 3),
            stk(st_p, 4), stk(st_s, 4), stk(st_p, 5), stk(st_s, 5),
            stk(st_p, 6), stk(st_s, 6))
```

```python
import functools
import math

import jax
import jax.numpy as jnp
import numpy as np
from jax import lax
from jax.experimental import pallas as pl
from jax.experimental.pallas import tpu as pltpu

F32 = jnp.float32
BF16 = jnp.bfloat16
I32 = jnp.int32

HEAD_DIM = 64
SCALE = HEAD_DIM ** -0.5
A_HEADS = 4
F_HEADS = 8
S_HEADS = 8
S_HEADDIM = 64
S_GROUPS = 2
HPG = S_HEADS // S_GROUPS
D_STATE = 64
CONV_W = 4
ROT_DIM = HEAD_DIM // 4
ROPE_THETA = 500000.0
A_WIDTH = A_HEADS * 2 * HEAD_DIM
F_WIDTH = F_HEADS * HEAD_DIM
S_WIDTH = S_HEADS * S_HEADDIM
CONV_DIM = S_WIDTH + 2 * S_GROUPS * D_STATE
N_BRANCH = 3
N_EXPERTS = 64
N_EXPERT_GROUPS = 8
TOPK_GROUPS = 4
TOP_K = 8
ROUTED_SCALE = 2.5
EPS = 1e-6
NEG = -1e30
LANES = 128
SMALL_W = LANES
VMEM_LIMIT = 56 * 1024 * 1024

_OFF = {}
_o = 0
for _n, _w in (("aq", A_WIDTH), ("ak", A_WIDTH), ("av", A_WIDTH), ("fq", F_WIDTH), ("fk", F_WIDTH),
               ("fv", F_WIDTH), ("sz", S_WIDTH), ("sxbc", CONV_DIM), ("small", SMALL_W)):
    _OFF[_n] = (_o, _o + _w)
    _o += _w
PACK_COLS = _o


def _cparams(sem):
    return pltpu.CompilerParams(dimension_semantics=sem, vmem_limit_bytes=VMEM_LIMIT)


def _dot(a, b):
    return jnp.dot(a, b, preferred_element_type=F32)


def _dot_nt(a, b):
    return lax.dot_general(a, b, (((1,), (1,)), ((), ())), preferred_element_type=F32)


def _silu(x):
    return x * (1.0 / (1.0 + jnp.exp(-x)))


def _sigmoid(x):
    return 1.0 / (1.0 + jnp.exp(-x))


def _softplus(x):
    return jnp.maximum(x, 0.0) + jnp.log1p(jnp.exp(-jnp.abs(x)))


def _norm_mod(x, nw, scale, shift):
    ms = jnp.mean(x * x, axis=-1, keepdims=True)
    y = x * lax.rsqrt(ms + EPS) * nw
    return y * (1.0 + scale) + shift


def _ada_kernel(c_ref, w_ref, b_ref, o_ref):
    c = _silu(c_ref[...]).astype(BF16)
    o_ref[...] = _dot(c, w_ref[...].astype(BF16)) + b_ref[...]


def _ada(c_all, w, b, tn=512):
    m, d = c_all.shape
    n = w.shape[1]
    return pl.pallas_call(
        _ada_kernel,
        grid=(n // tn,),
        in_specs=[pl.BlockSpec((m, d), lambda j: (0, 0)),
                  pl.BlockSpec((d, tn), lambda j: (0, j)),
                  pl.BlockSpec((1, tn), lambda j: (0, j))],
        out_specs=pl.BlockSpec((m, tn), lambda j: (0, j)),
        out_shape=jax.ShapeDtypeStruct((m, n), F32),
        compiler_params=_cparams(("arbitrary",)),
        name="ada",
    )(c_all, w, b.reshape(1, n))


def _proj_kernel(x_ref, sh_ref, sc_ref, nw_ref, w_ref, bias_ref, rc_ref, rn_ref, rp_ref,
                 qa_ref, ka_ref, kab_ref, va_ref, vab_ref,
                 qf_ref, kf_ref, kfb_ref, vf_ref, vfb_ref,
                 sz_ref, sxbc_ref, small_ref, lf_ref, cum_ref, run_ref, *, tiles_per_batch):
    i = pl.program_id(0)
    h = _norm_mod(x_ref[...], nw_ref[...], sc_ref[...], sh_ref[...]).astype(BF16)
    tm = h.shape[0]

    def col(name):
        a, b = _OFF[name]
        return _dot(h, w_ref[:, a:b])

    rc = jnp.concatenate([rc_ref[...]] * (A_WIDTH // LANES), axis=1)
    rn = jnp.concatenate([rn_ref[...]] * (A_WIDTH // LANES), axis=1)
    rp = jnp.concatenate([rp_ref[...]] * (A_WIDTH // LANES), axis=1)

    def rope(v):
        half = ROT_DIM // 2
        parts = []
        for j in range(A_WIDTH // LANES):
            vj = v[:, j * LANES:(j + 1) * LANES]
            parts.append((pltpu.roll(vj, LANES - half, axis=1), pltpu.roll(vj, half, axis=1)))
        left = jnp.concatenate([p[0] for p in parts], axis=1)
        right = jnp.concatenate([p[1] for p in parts], axis=1)
        return v * rc + left * rn + right * rp

    aq = rope(col("aq"))
    qa_ref[...] = (aq * SCALE).astype(BF16)
    ak = rope(col("ak"))
    ka_ref[...] = ak
    kab_ref[...] = ak.astype(BF16)
    av = col("av")
    va_ref[...] = av
    vab_ref[...] = av.astype(BF16)
    qf_ref[...] = (col("fq") * SCALE).astype(BF16)
    fk = col("fk")
    kf_ref[...] = fk
    kfb_ref[...] = fk.astype(BF16)
    fv = col("fv")
    vf_ref[...] = fv
    vfb_ref[...] = fv.astype(BF16)
    sz_ref[...] = col("sz")
    sxbc_ref[...] = col("sxbc")
    small = col("small")
    small_ref[...] = small
    z = small + bias_ref[...]
    lf = -_softplus(-z)
    lf_ref[...] = lf

    @pl.when(i % tiles_per_batch == 0)
    def _():
        run_ref[...] = jnp.zeros_like(run_ref)

    r = lax.broadcasted_iota(I32, (tm, tm), 0)
    c = lax.broadcasted_iota(I32, (tm, tm), 1)
    tri = jnp.where(c <= r, 1.0, 0.0).astype(F32)
    cum = jnp.dot(tri, lf, preferred_element_type=F32, precision=lax.Precision.HIGHEST) + run_ref[...]
    cum_ref[...] = cum
    run_ref[...] = cum[tm - 1:tm, :]


def _proj(x, shift, scale, nw, w_pack, bias_small, rope_tabs, *, tm, tiles_per_batch, per_row_mod):
    n, d = x.shape
    nt = n // tm
    if per_row_mod:
        mod_spec = pl.BlockSpec((tm, d), lambda i: (i, 0))
    else:
        mod_spec = pl.BlockSpec((None, 1, d), lambda i: (i // tiles_per_batch, 0, 0))
    tab_spec = pl.BlockSpec((tm, LANES), lambda i: (i % tiles_per_batch, 0))
    row = lambda w: pl.BlockSpec((tm, w), lambda i: (i, 0))
    f = lambda w, dt: jax.ShapeDtypeStruct((n, w), dt)
    out_shapes = [f(A_WIDTH, BF16), f(A_WIDTH, F32), f(A_WIDTH, BF16), f(A_WIDTH, F32), f(A_WIDTH, BF16),
                  f(F_WIDTH, BF16), f(F_WIDTH, F32), f(F_WIDTH, BF16), f(F_WIDTH, F32), f(F_WIDTH, BF16),
                  f(S_WIDTH, F32), f(CONV_DIM, F32), f(SMALL_W, F32), f(SMALL_W, F32), f(SMALL_W, F32)]
    out_specs = [row(s.shape[1]) for s in out_shapes]
    return pl.pallas_call(
        functools.partial(_proj_kernel, tiles_per_batch=tiles_per_batch),
        grid=(nt,),
        in_specs=[row(d), mod_spec, mod_spec, pl.BlockSpec((1, d), lambda i: (0, 0)),
                  pl.BlockSpec((d, PACK_COLS), lambda i: (0, 0)),
                  pl.BlockSpec((1, SMALL_W), lambda i: (0, 0)),
                  tab_spec, tab_spec, tab_spec],
        out_specs=out_specs,
        out_shape=out_shapes,
        scratch_shapes=[pltpu.VMEM((1, SMALL_W), F32)],
        compiler_params=_cparams(("arbitrary",)),
        name="proj",
    )(x, shift, scale, nw, w_pack, bias_small, *rope_tabs)


_ROW_OFF = {}
_o = 0
for _n, _w in (("ak", A_WIDTH), ("av", A_WIDTH), ("fk", F_WIDTH), ("sz", S_WIDTH), ("sxbc", CONV_DIM),
               ("small", SMALL_W)):
    _ROW_OFF[_n] = (_o, _o + _w)
    _o += _w
ROW_COLS = _o
_T_OFF = {n: (j * A_WIDTH, (j + 1) * A_WIDTH) for j, n in enumerate(("aq", "ak", "av", "fq", "fk", "fv"))}
T_ROWS = 6 * A_WIDTH


def _proj_t_kernel(x_ref, sh_ref, sc_ref, nw_ref, wr_ref, wt_ref, bias_ref, rc_ref, rn_ref, rp_ref,
                   rct_ref, rnt_ref, rpt_ref, sel_ref,
                   qat_ref, kab_ref, kat_ref, va_ref, vat_ref,
                   qft_ref, kfb_ref, ckf_ref, kft_ref, vft32_ref, vft_ref,
                   sz_ref, sxbc_ref, small_ref, lf_ref, cum_ref, run_ref, *, tiles_per_batch):
    i = pl.program_id(0)
    h = _norm_mod(x_ref[...], nw_ref[...], sc_ref[...], sh_ref[...]).astype(BF16)
    tm = h.shape[0]
    half = ROT_DIM // 2
    reps = A_WIDTH // LANES

    def col(name):
        a, b = _ROW_OFF[name]
        return _dot(h, wr_ref[:, a:b])

    def colt(name):
        a, b = _T_OFF[name]
        return _dot_nt(wt_ref[a:b, :], h)

    rc = jnp.concatenate([rc_ref[...]] * reps, axis=1)
    rn = jnp.concatenate([rn_ref[...]] * reps, axis=1)
    rp = jnp.concatenate([rp_ref[...]] * reps, axis=1)
    rct = jnp.concatenate([rct_ref[...]] * reps, axis=0)
    rnt = jnp.concatenate([rnt_ref[...]] * reps, axis=0)
    rpt = jnp.concatenate([rpt_ref[...]] * reps, axis=0)

    def rope(v):
        parts = []
        for j in range(reps):
            vj = v[:, j * LANES:(j + 1) * LANES]
            parts.append((pltpu.roll(vj, LANES - half, axis=1), pltpu.roll(vj, half, axis=1)))
        left = jnp.concatenate([p[0] for p in parts], axis=1)
        right = jnp.concatenate([p[1] for p in parts], axis=1)
        return v * rc + left * rn + right * rp

    def rope_t(v):
        up = jnp.concatenate([v[half:], v[:half]], axis=0)
        down = jnp.concatenate([v[-half:], v[:-half]], axis=0)
        return v * rct + up * rnt + down * rpt

    qat_ref[...] = (rope_t(colt("aq")) * SCALE).astype(BF16)
    kab_ref[...] = rope(col("ak")).astype(BF16)
    kat_ref[...] = rope_t(colt("ak"))
    va_ref[...] = col("av")
    vat_ref[...] = colt("av").astype(BF16)
    qft_ref[...] = (colt("fq") * SCALE).astype(BF16)
    kfb_ref[...] = col("fk").astype(BF16)
    kft_ref[...] = colt("fk")
    fvt = colt("fv")
    vft32_ref[...] = fvt
    vft_ref[...] = fvt.astype(BF16)
    sz_ref[...] = col("sz")
    sxbc_ref[...] = col("sxbc")
    small = col("small")
    small_ref[...] = small
    z = small + bias_ref[...]
    lf = -_softplus(-z)
    lf_ref[...] = lf

    @pl.when(i % tiles_per_batch == 0)
    def _():
        run_ref[...] = jnp.zeros_like(run_ref)

    r = lax.broadcasted_iota(I32, (tm, tm), 0)
    c = lax.broadcasted_iota(I32, (tm, tm), 1)
    tri = jnp.where(c <= r, 1.0, 0.0).astype(F32)
    cum = jnp.dot(tri, lf, preferred_element_type=F32, precision=lax.Precision.HIGHEST) + run_ref[...]
    cum_ref[...] = cum
    run_ref[...] = cum[tm - 1:tm, :]
    hi = cum.astype(BF16)
    r1 = cum - hi.astype(F32)
    mid = r1.astype(BF16)
    lo = (r1 - mid.astype(F32)).astype(BF16)
    ckf_ref[...] = _dot(jnp.concatenate([hi, mid, lo], axis=1), sel_ref[...]).astype(BF16)


def _ck_select():
    sel = np.zeros((CK_PIECES * LANES, F_WIDTH), np.float32)
    for piece in range(CK_PIECES):
        for hd in range(F_HEADS):
            sel[piece * LANES + hd, (hd // 2) * LANES + (hd % 2) * CK_PIECES + piece] = 1.0
    return jnp.asarray(sel, BF16)


def _proj_t(x, shift, scale, nw, w_row, w_t, bias_small, rope_tabs, rope_tabs_t, sel, *, nb, seq, tm):
    n, d = x.shape
    nt = n // tm
    tpb = seq // tm
    mod_spec = pl.BlockSpec((None, 1, d), lambda i: (i // tpb, 0, 0))
    tab_spec = pl.BlockSpec((tm, LANES), lambda i: (i % tpb, 0))
    tabt_spec = pl.BlockSpec((LANES, tm), lambda i: (0, i % tpb))
    row = lambda w: pl.BlockSpec((tm, w), lambda i: (i, 0))
    tile_t = pl.BlockSpec((None, A_WIDTH, tm), lambda i: (i, 0, 0))
    leaf_t = pl.BlockSpec((None, A_WIDTH, tm), lambda i: (i // tpb, 0, i % tpb))
    full = lambda a: pl.BlockSpec(a.shape, lambda i: (0,) * a.ndim)
    f = lambda w, dt: jax.ShapeDtypeStruct((n, w), dt)
    tiles = lambda dt: jax.ShapeDtypeStruct((nt, A_WIDTH, tm), dt)
    leaf = jax.ShapeDtypeStruct((nb, A_WIDTH, seq), F32)
    out_shapes = [tiles(BF16), f(A_WIDTH, BF16), leaf, f(A_WIDTH, F32), tiles(BF16),
                  tiles(BF16), f(F_WIDTH, BF16), f(F_WIDTH, BF16), leaf, leaf, tiles(BF16),
                  f(S_WIDTH, F32), f(CONV_DIM, F32), f(SMALL_W, F32), f(SMALL_W, F32), f(SMALL_W, F32)]
    out_specs = [tile_t, row(A_WIDTH), leaf_t, row(A_WIDTH), tile_t,
                 tile_t, row(F_WIDTH), row(F_WIDTH), leaf_t, leaf_t, tile_t,
                 row(S_WIDTH), row(CONV_DIM), row(SMALL_W), row(SMALL_W), row(SMALL_W)]
    return pl.pallas_call(
        functools.partial(_proj_t_kernel, tiles_per_batch=tpb),
        grid=(nt,),
        in_specs=[row(d), mod_spec, mod_spec, pl.BlockSpec((1, d), lambda i: (0, 0)), full(w_row), full(w_t),
                  pl.BlockSpec((1, SMALL_W), lambda i: (0, 0)), tab_spec, tab_spec, tab_spec,
                  tabt_spec, tabt_spec, tabt_spec, full(sel)],
        out_specs=out_specs,
        out_shape=out_shapes,
        scratch_shapes=[pltpu.VMEM((1, SMALL_W), F32)],
        compiler_params=_cparams(("arbitrary",)),
        name="proj_t",
    )(x, shift, scale, nw, w_row, w_t, bias_small, *rope_tabs, *rope_tabs_t, sel)


def _lam(lp_ref):
    lp = lp_ref[...]
    s01 = jnp.sum(lp[0:1, :] * lp[1:2, :], axis=-1, keepdims=True)
    s23 = jnp.sum(lp[2:3, :] * lp[3:4, :], axis=-1, keepdims=True)
    return jnp.exp(s01) - jnp.exp(s23)


CK_PIECES = 3


def _attn_kernel(qt_ref, k_ref, *rest, fox, lam_init, tq):
    if fox:
        ckf_ref, vt_ref, o_ref, q2_ref, m_ref, l_ref, acc_ref = rest[:7]
    else:
        vt_ref, lp_ref, sub_ref, o_ref, q2_ref, m_ref, l_ref, acc_ref = rest[:8]
    s_refs, p_refs, a_refs = rest[-6:-4], rest[-4:-2], rest[-2:]
    qi = pl.program_id(2)
    tk = tq
    q32 = qt_ref[...].astype(F32)
    frow = lax.broadcasted_iota(I32, (LANES, tq), 0)
    q2_ref[0:LANES, :] = jnp.concatenate([jnp.where(frow < HEAD_DIM, q32, 0.0),
                                          jnp.where(frow >= HEAD_DIM, q32, 0.0)], axis=1).astype(BF16)
    if fox:
        er = lax.broadcasted_iota(I32, (LANES, 2 * tq), 0)
        ec = lax.broadcasted_iota(I32, (LANES, 2 * tq), 1)
        part = jnp.where(ec < tq, 0, 1)
        hit = jnp.logical_and(er >= part * CK_PIECES, er < (part + 1) * CK_PIECES)
        q2_ref[LANES:2 * LANES, :] = jnp.where(hit, -1.0, 0.0).astype(BF16)
    m_ref[...] = jnp.full(m_ref.shape, -jnp.inf, F32)
    l_ref[...] = jnp.zeros(l_ref.shape, F32)
    acc_ref[...] = jnp.zeros(acc_ref.shape, F32)
    p_refs[1][...] = jnp.zeros(p_refs[1].shape, BF16)
    a_refs[1][...] = jnp.ones(a_refs[1].shape, F32)

    def scores(jb, slot, masked):
        k0 = pl.multiple_of(jb * tk, tk)
        kb = k_ref[pl.ds(k0, tk), :]
        if fox:
            kb = jnp.concatenate([kb, ckf_ref[pl.ds(k0, tk), :]], axis=1)
        s = _dot(kb, q2_ref[...])
        if masked:
            r = lax.broadcasted_iota(I32, (tk, 2 * tq), 0)
            c = lax.broadcasted_iota(I32, (tk, 2 * tq), 1)
            c = jnp.where(c >= tq, c - tq, c)
            s = jnp.where(r <= c, s, NEG)
        s_refs[slot][...] = s

    def softmax(slot):
        s = s_refs[slot][...]
        m_old = m_ref[...]
        m_new = jnp.maximum(m_old, jnp.max(s, axis=0, keepdims=True))
        alpha = jnp.exp(m_old - m_new)
        p = jnp.exp(s - m_new)
        l_ref[...] = alpha * l_ref[...] + jnp.sum(p, axis=0, keepdims=True)
        m_ref[...] = m_new
        p_refs[slot][...] = p.astype(BF16)
        a_refs[slot][...] = alpha

    def values(jb, slot):
        acc_ref[...] = a_refs[slot][...] * acc_ref[...] + _dot(vt_ref[jb], p_refs[slot][...])

    @pl.when(qi == 0)
    def _():
        scores(0, 0, True)
        softmax(0)
        values(0, 0)

    n_pairs = jnp.maximum(qi - 1, 0) // 2

    @pl.when(qi >= 1)
    def _():
        scores(0, 0, False)

    def pair(t, carry):
        scores(2 * t + 1, 1, False)
        softmax(0)
        values(jnp.maximum(2 * t - 1, 0), 1)
        scores(2 * t + 2, 0, False)
        softmax(1)
        values(2 * t, 0)
        return carry

    lax.fori_loop(0, n_pairs, pair, 0)
    done = 2 * n_pairs

    @pl.when(qi % 2 == 1)
    def _():
        scores(done + 1, 1, True)
        softmax(0)
        values(jnp.maximum(done - 1, 0), 1)
        softmax(1)
        values(done, 0)
        values(done + 1, 1)

    @pl.when(jnp.logical_and(qi % 2 == 0, qi >= 2))
    def _():
        scores(done + 1, 1, False)
        softmax(0)
        values(jnp.maximum(done - 1, 0), 1)
        scores(done + 2, 0, True)
        softmax(1)
        values(done, 0)
        softmax(0)
        values(done + 1, 1)
        values(done + 2, 0)

    o = acc_ref[...] / l_ref[...]
    if fox:
        ot = jnp.concatenate([o[0:HEAD_DIM, 0:tq], o[HEAD_DIM:LANES, tq:2 * tq]], axis=0)
    else:
        lam = _lam(lp_ref) + lam_init
        ot = o[:, 0:tq] - lam * o[:, tq:2 * tq]
        ms = jnp.mean(ot * ot, axis=0, keepdims=True)
        ot = ot * lax.rsqrt(ms + EPS)
    out = ot.T
    if not fox:
        out = out * sub_ref[...] * (1.0 - lam_init)
    o_ref[...] = out.astype(o_ref.dtype)


def _attn(qt, k, ckf, vt, lam_params, subln, lam_init, *, fox, nb, seq, tq):
    n = k.shape[0]
    nq = seq // tq
    groups = k.shape[1] // LANES
    qspec = pl.BlockSpec((None, LANES, tq), lambda b, h, i: (b * nq + i, h, 0))
    kspec = pl.BlockSpec((seq, LANES), lambda b, h, i: (b, h))
    vspec = pl.BlockSpec((nq, LANES, tq), lambda b, h, i: (b, h, 0))
    ospec = pl.BlockSpec((tq, LANES), lambda b, h, i: (b * nq + i, h))
    if fox:
        in_specs, args = [qspec, kspec, kspec, vspec], (qt, k, ckf, vt)
    else:
        in_specs = [qspec, kspec, vspec, pl.BlockSpec((4, HEAD_DIM), lambda b, h, i: (0, 0)),
                    pl.BlockSpec((1, LANES), lambda b, h, i: (0, 0))]
        args = (qt, k, vt, lam_params, subln.reshape(1, LANES))
    return pl.pallas_call(
        functools.partial(_attn_kernel, fox=fox, lam_init=lam_init, tq=tq),
        grid=(nb, groups, nq),
        in_specs=in_specs,
        out_specs=ospec,
        out_shape=jax.ShapeDtypeStruct((n, groups * LANES), BF16),
        scratch_shapes=[pltpu.VMEM(((2 if fox else 1) * LANES, 2 * tq), BF16),
                        pltpu.VMEM((1, 2 * tq), F32), pltpu.VMEM((1, 2 * tq), F32),
                        pltpu.VMEM((LANES, 2 * tq), F32),
                        pltpu.VMEM((tq, 2 * tq), F32), pltpu.VMEM((tq, 2 * tq), F32),
                        pltpu.VMEM((tq, 2 * tq), BF16), pltpu.VMEM((tq, 2 * tq), BF16),
                        pltpu.VMEM((1, 2 * tq), F32), pltpu.VMEM((1, 2 * tq), F32)],
        compiler_params=_cparams(("arbitrary", "arbitrary", "arbitrary")),
        name="fox_attn" if fox else "diff_attn",
    )(*args)


def _ssd_kernel(xbc_ref, z_ref, small_ref, cw_ref, cb_ref, dtb_ref, alog_ref, dsk_ref, nw_ref,
                o_ref, hout_ref, cs_ref, ext_ref, h_ref, *, chunk):
    c = pl.program_id(1)
    nc = pl.num_programs(1)
    L = chunk
    pad = 8

    @pl.when(c == 0)
    def _():
        ext_ref[0:pad, :] = jnp.zeros((pad, CONV_DIM), F32)
        h_ref[...] = jnp.zeros_like(h_ref)

    xin = xbc_ref[...]
    ext_ref[pad:pad + L, :] = xin
    y = cb_ref[...]
    for j in range(CONV_W - 1):
        sft = CONV_W - 1 - j
        y = y + cw_ref[j:j + 1, :] * ext_ref[pad - sft:pad - sft + L, :]
    y = y + cw_ref[CONV_W - 1:CONV_W, :] * xin
    tail = xin[L - pad:L, :]
    ext_ref[0:pad, :] = tail
    cs_ref[...] = tail
    xbc = _silu(y)
    xs = xbc[:, :S_WIDTH]

    dt_raw = small_ref[...] + dtb_ref[...]
    dt = _softplus(dt_raw)
    da = dt * (-jnp.exp(alog_ref[...]))
    r = lax.broadcasted_iota(I32, (L, L), 0)
    cc = lax.broadcasted_iota(I32, (L, L), 1)
    causal = cc <= r
    tri = jnp.where(causal, 1.0, 0.0).astype(F32)
    a_cs = jnp.dot(tri, da, preferred_element_type=F32, precision=lax.Precision.HIGHEST)
    a_cs_t = a_cs.T
    a_end = a_cs[L - 1:L, :]
    to_end = jnp.exp(a_end - a_cs)
    exp_cs = jnp.exp(a_cs)
    exp_end = jnp.exp(a_end)

    xds = [xs[:, hd * S_HEADDIM:(hd + 1) * S_HEADDIM] * dt[:, 8 + hd:9 + hd] for hd in range(S_HEADS)]
    xw_t = jnp.concatenate([xds[hd] * to_end[:, 8 + hd:9 + hd] for hd in range(S_HEADS)], axis=1).T
    ys = []
    for g in range(S_GROUPS):
        bm = xbc[:, S_WIDTH + g * D_STATE:S_WIDTH + (g + 1) * D_STATE].astype(BF16)
        cm = xbc[:, S_WIDTH + (S_GROUPS + g) * D_STATE:S_WIDTH + (S_GROUPS + g + 1) * D_STATE].astype(BF16)
        cb = _dot_nt(cm, bm)
        for rr in range(HPG):
            hd = g * HPG + rr
            col = 8 + hd
            x_h = xs[:, hd * S_HEADDIM:(hd + 1) * S_HEADDIM]
            xd = xds[hd]
            seg = a_cs[:, col:col + 1] - a_cs_t[col:col + 1, :]
            decay = jnp.where(causal, jnp.exp(jnp.where(causal, seg, 0.0)), 0.0)
            y_diag = _dot((cb * decay).astype(BF16), xd.astype(BF16))
            h_prev = h_ref[hd]
            y_off = _dot_nt(cm, h_prev.astype(BF16)) * exp_cs[:, col:col + 1]
            states = _dot(xw_t[hd * S_HEADDIM:(hd + 1) * S_HEADDIM, :].astype(BF16), bm)
            h_ref[hd] = h_prev * exp_end[:, col:col + 1] + states
            ys.append(y_diag + y_off + x_h * dsk_ref[:, col:col + 1])
    yy = jnp.concatenate(ys, axis=1) * _silu(z_ref[...])
    gw = S_WIDTH // S_GROUPS
    outs = []
    for g in range(S_GROUPS):
        yg = yy[:, g * gw:(g + 1) * gw]
        ms = jnp.mean(yg * yg, axis=-1, keepdims=True)
        outs.append(yg * lax.rsqrt(ms + EPS) * nw_ref[:, g * gw:(g + 1) * gw])
    o_ref[...] = jnp.concatenate(outs, axis=1).astype(o_ref.dtype)

    @pl.when(c == nc - 1)
    def _():
        hout_ref[...] = h_ref[...]


def _lane_pack(v, off=8):
    return jnp.zeros((1, SMALL_W), F32).at[0, off:off + v.shape[0]].set(v.astype(F32))


def _ssd_prompt(sxbc, sz, small, p, *, nb, seq, chunk):
    n = sxbc.shape[0]
    ncn = seq // chunk
    row = lambda w: pl.BlockSpec((chunk, w), lambda b, c: (b * ncn + c, 0))
    full = lambda a: pl.BlockSpec(a.shape, lambda b, c: (0,) * a.ndim)
    cw = p['ssm_conv_w']
    cb = p['ssm_conv_b'].reshape(1, CONV_DIM)
    dtb = _lane_pack(p['ssm_dt_bias'])
    alog = _lane_pack(p['ssm_a_log'])
    dsk = _lane_pack(p['ssm_d'])
    nw = p['ssm_norm'].reshape(1, S_WIDTH)
    o, hout, cs = pl.pallas_call(
        functools.partial(_ssd_kernel, chunk=chunk),
        grid=(nb, ncn),
        in_specs=[row(CONV_DIM), row(S_WIDTH), row(SMALL_W), full(cw), full(cb), full(dtb), full(alog),
                  full(dsk), full(nw)],
        out_specs=[row(S_WIDTH),
                   pl.BlockSpec((None, S_HEADS, S_HEADDIM, D_STATE), lambda b, c: (b, 0, 0, 0)),
                   pl.BlockSpec((None, 8, CONV_DIM), lambda b, c: (b, 0, 0))],
        out_shape=[jax.ShapeDtypeStruct((n, S_WIDTH), BF16),
                   jax.ShapeDtypeStruct((nb, S_HEADS, S_HEADDIM, D_STATE), F32),
                   jax.ShapeDtypeStruct((nb, 8, CONV_DIM), F32)],
        scratch_shapes=[pltpu.VMEM((8 + chunk, CONV_DIM), F32),
                        pltpu.VMEM((S_HEADS, S_HEADDIM, D_STATE), F32)],
        compiler_params=_cparams(("arbitrary", "arbitrary")),
        name="ssd",
    )(sxbc, sz, small, cw, cb, dtb, alog, dsk, nw)
    return o, hout, cs[:, 8 - (CONV_W - 1):, :]


def _ssd_step_kernel(xbc_ref, buf_ref, z_ref, small_ref, h_ref, cw_ref, cb_ref, dtb_ref, alog_ref, dsk_ref,
                     nw_ref, o_ref, hout_ref, cs_ref):
    xin = xbc_ref[...]
    buf = buf_ref[...]
    y = cb_ref[...]
    for j in range(CONV_W - 1):
        y = y + cw_ref[j:j + 1, :] * buf[j:j + 1, :]
    y = y + cw_ref[CONV_W - 1:CONV_W, :] * xin
    cs_ref[...] = jnp.concatenate([buf[1:CONV_W - 1, :], xin], axis=0)
    xbc = _silu(y)
    xs = xbc[:, :S_WIDTH]
    dt_raw = small_ref[...] + dtb_ref[...]
    dt = _softplus(dt_raw)
    dec = jnp.exp(dt * (-jnp.exp(alog_ref[...])))
    rows = S_HEADS * S_HEADDIM
    er = lax.broadcasted_iota(I32, (SMALL_W, rows), 0)
    ec = lax.broadcasted_iota(I32, (SMALL_W, rows), 1)
    expand = jnp.where(er - 8 == ec // S_HEADDIM, 1.0, 0.0).astype(F32)
    stacked = jnp.concatenate([dt, dec, dsk_ref[...], jnp.zeros((5, SMALL_W), F32)], axis=0)
    per_pos = jnp.dot(stacked, expand, preferred_element_type=F32, precision=lax.Precision.HIGHEST)
    dt_row, dec_row, dsk_row = per_pos[0:1, :], per_pos[1:2, :], per_pos[2:3, :]
    xd_row = xs * dt_row

    def to_col(v):
        return jnp.broadcast_to(v, (LANES, rows)).T

    xd_col = to_col(xd_row)[:, :D_STATE]
    dec_col = to_col(dec_row)[:, :D_STATE]
    half = rows // S_GROUPS
    bsel = lax.broadcasted_iota(I32, (rows, D_STATE), 0) < half
    b0 = xbc[:, S_WIDTH:S_WIDTH + D_STATE]
    b1 = xbc[:, S_WIDTH + D_STATE:S_WIDTH + 2 * D_STATE]
    c0 = xbc[:, S_WIDTH + 2 * D_STATE:S_WIDTH + 3 * D_STATE]
    c1 = xbc[:, S_WIDTH + 3 * D_STATE:S_WIDTH + 4 * D_STATE]
    bmat = jnp.where(bsel, jnp.broadcast_to(b0, (rows, D_STATE)), jnp.broadcast_to(b1, (rows, D_STATE)))
    cmat = jnp.where(bsel, jnp.broadcast_to(c0, (rows, D_STATE)), jnp.broadcast_to(c1, (rows, D_STATE)))
    h = h_ref[...].reshape(rows, D_STATE)
    hn = h * dec_col + xd_col * bmat
    hout_ref[...] = hn.reshape(S_HEADS, S_HEADDIM, D_STATE)
    ycol = jnp.sum(hn.astype(BF16).astype(F32) * cmat.astype(BF16).astype(F32), axis=-1, keepdims=True)
    yrow = jnp.broadcast_to(ycol, (rows, LANES)).T[0:1, :]
    yy = (yrow + xs * dsk_row) * _silu(z_ref[...])
    gw = S_WIDTH // S_GROUPS
    outs = []
    for g in range(S_GROUPS):
        yg = yy[:, g * gw:(g + 1) * gw]
        ms = jnp.mean(yg * yg, axis=-1, keepdims=True)
        outs.append(yg * lax.rsqrt(ms + EPS) * nw_ref[:, g * gw:(g + 1) * gw])
    o_ref[...] = jnp.concatenate(outs, axis=1).astype(o_ref.dtype)


def _ssd_sample(sxbc, conv_buf, sz, small, h0, p):
    nb = sxbc.shape[0]
    cw = p['ssm_conv_w']
    cb = p['ssm_conv_b'].reshape(1, CONV_DIM)
    dtb = _lane_pack(p['ssm_dt_bias'])
    alog = _lane_pack(p['ssm_a_log'])
    dsk = _lane_pack(p['ssm_d'])
    nw = p['ssm_norm'].reshape(1, S_WIDTH)
    full = lambda a: pl.BlockSpec(a.shape, lambda b: (0,) * a.ndim)
    r3 = lambda w: pl.BlockSpec((None, 1, w), lambda b: (b, 0, 0))
    o, hout, cs = pl.pallas_call(
        _ssd_step_kernel,
        grid=(nb,),
        in_specs=[r3(CONV_DIM), pl.BlockSpec((None, CONV_W - 1, CONV_DIM), lambda b: (b, 0, 0)),
                  r3(S_WIDTH), r3(SMALL_W),
                  pl.BlockSpec((None, S_HEADS, S_HEADDIM, D_STATE), lambda b: (b, 0, 0, 0)),
                  full(cw), full(cb), full(dtb), full(alog), full(dsk), full(nw)],
        out_specs=[r3(S_WIDTH),
                   pl.BlockSpec((None, S_HEADS, S_HEADDIM, D_STATE), lambda b: (b, 0, 0, 0)),
                   pl.BlockSpec((None, CONV_W - 1, CONV_DIM), lambda b: (b, 0, 0))],
        out_shape=[jax.ShapeDtypeStruct((nb, 1, S_WIDTH), BF16),
                   jax.ShapeDtypeStruct((nb, S_HEADS, S_HEADDIM, D_STATE), F32),
                   jax.ShapeDtypeStruct((nb, CONV_W - 1, CONV_DIM), F32)],
        compiler_params=_cparams(("arbitrary",)),
        name="ssd_step",
    )(sxbc.reshape(nb, 1, CONV_DIM), conv_buf, sz.reshape(nb, 1, S_WIDTH), small.reshape(nb, 1, SMALL_W),
      h0, cw, cb, dtb, alog, dsk, nw)
    return o.reshape(nb, S_WIDTH), hout, cs


def _decode_kernel(pt_ref, qa_ref, kan_ref, van_ref, qf_ref, kfn_ref, vfn_ref, lfn_ref, lp_ref, sub_ref,
                   *refs, lam_init, pages_per_step, n_steps):
    P = pages_per_step
    kd_refs = refs[0:P]
    kf_refs = refs[P:2 * P]
    lf_refs = refs[2 * P:3 * P]
    vd_refs = refs[3 * P:4 * P]
    vf_refs = refs[4 * P:5 * P]
    oa_ref, of_ref = refs[5 * P:5 * P + 2]
    (qd_ref, qfm_ref, sd_ref, sf_ref, md_ref, mf_ref, ld_ref, lfs_ref, accd_ref, accf_ref, suf_ref,
     ssd_ref, ssf_ref) = refs[5 * P + 2:]
    j = pl.program_id(1)
    W = A_WIDTH
    n_pages = n_steps * P
    bf = lambda a: a.astype(BF16).astype(F32)

    @pl.when(j == 0)
    def _():
        seg16 = lax.broadcasted_iota(I32, (16, W), 1) // HEAD_DIM
        row16 = lax.broadcasted_iota(I32, (16, W), 0)
        want = jnp.where(row16 < 8, 2 * row16, 2 * (row16 - 8) + 1)
        keep = jnp.logical_and(seg16 == want, (row16 % 8) < A_HEADS)
        qd_ref[...] = jnp.where(keep, jnp.broadcast_to(qa_ref[...].astype(F32), (16, W)), 0.0).astype(BF16)
        seg8 = lax.broadcasted_iota(I32, (8, W), 1) // HEAD_DIM
        row8 = lax.broadcasted_iota(I32, (8, W), 0)
        qfm_ref[...] = jnp.where(seg8 == row8, jnp.broadcast_to(qf_ref[...].astype(F32), (8, W)), 0.0).astype(BF16)
        md_ref[...] = jnp.full(md_ref.shape, -jnp.inf, F32)
        mf_ref[...] = jnp.full(mf_ref.shape, -jnp.inf, F32)
        accd_ref[...] = jnp.zeros_like(accd_ref)
        accf_ref[...] = jnp.zeros_like(accf_ref)
        suf_ref[...] = jnp.zeros_like(suf_ref)

    lane = lax.broadcasted_iota(I32, (8, PAGE), 1)

    @pl.when(j < n_steps)
    def _():
        for pp in range(P):
            g = j * P + pp
            sd = _dot(qd_ref[...], kd_refs[pp][...].astype(BF16))
            sd_ref[g] = sd
            md_ref[...] = jnp.maximum(md_ref[...], jnp.max(sd, axis=-1, keepdims=True))
            lf = lf_refs[pp][...]
            inc = lf
            d = 1
            while d < PAGE:
                inc = inc + jnp.where(lane + d < PAGE, pltpu.roll(inc, PAGE - d, axis=1), 0.0)
                d *= 2
            sfx = inc - lf + suf_ref[...]
            sf = (_dot(qfm_ref[...], kf_refs[pp][...].astype(BF16)) + lfn_ref[...]) + sfx
            sf_ref[g] = sf
            mf_ref[...] = jnp.maximum(mf_ref[...], jnp.max(sf, axis=-1, keepdims=True))
            suf_ref[...] = suf_ref[...] + jnp.sum(lf, axis=-1, keepdims=True)

    @pl.when(j == n_steps)
    def _():
        s_self_d = jnp.sum(qd_ref[...].astype(F32) * bf(kan_ref[...]), axis=-1, keepdims=True)
        dnew = lfn_ref[...]
        s_self_f = (jnp.sum(qfm_ref[...].astype(F32) * bf(kfn_ref[...]), axis=-1, keepdims=True) + dnew) - dnew
        m_d = jnp.maximum(md_ref[...], s_self_d)
        m_f = jnp.maximum(mf_ref[...], s_self_f)
        e_self_d = jnp.exp(s_self_d - m_d)
        e_self_f = jnp.exp(s_self_f - m_f)

        def body(t, carry):
            l_d, l_f = carry
            for u in range(P):
                g = t * P + u
                e_d = jnp.exp(sd_ref[g] - m_d)
                sd_ref[g] = e_d
                e_f = jnp.exp(sf_ref[g] - m_f)
                sf_ref[g] = e_f
                l_d = l_d + e_d
                l_f = l_f + e_f
            return (l_d, l_f)

        l_d, l_f = lax.fori_loop(0, n_steps, body, (jnp.zeros((16, PAGE), F32), jnp.zeros((8, PAGE), F32)))
        l_d = jnp.sum(l_d, axis=-1, keepdims=True)
        l_f = jnp.sum(l_f, axis=-1, keepdims=True)
        ld_ref[...] = l_d + e_self_d
        lfs_ref[...] = l_f + e_self_f
        ssd_ref[...] = e_self_d
        ssf_ref[...] = e_self_f

    @pl.when(j >= n_steps)
    def _():
        lam = _lam(lp_ref) + lam_init
        l_d = ld_ref[...]
        l_f = lfs_ref[...]
        for pp in range(P):
            g = (j - n_steps) * P + pp
            e_d = sd_ref[g]
            a = e_d[0:8] / l_d[0:8] - lam * (e_d[8:16] / l_d[8:16])
            ab = a.astype(BF16)
            vd = vd_refs[pp]
            pv = jnp.concatenate([_dot(ab, vd[:, hh, :].astype(BF16)) for hh in range(A_HEADS)], axis=1)
            accd_ref[...] = accd_ref[...] + pv
            accf_ref[...] = accf_ref[...] + _dot_nt((sf_ref[g] / l_f).astype(BF16), vf_refs[pp][...].astype(BF16))

    @pl.when(j == 2 * n_steps - 1)
    def _():
        lam = _lam(lp_ref) + lam_init
        pr = ssd_ref[...] / ld_ref[...]
        a_self = bf(pr[0:8] - lam * pr[8:16])
        od = accd_ref[...] + a_self * bf(van_ref[...])
        hsel = lax.broadcasted_iota(I32, (8, W), 1) // (2 * HEAD_DIM)
        rowi = lax.broadcasted_iota(I32, (8, W), 0)
        o = jnp.sum(jnp.where(rowi == hsel, od, 0.0), axis=0, keepdims=True)
        outs = []
        for hh in range(A_HEADS):
            og = o[:, hh * 2 * HEAD_DIM:(hh + 1) * 2 * HEAD_DIM]
            ms = jnp.mean(og * og, axis=-1, keepdims=True)
            outs.append(og * lax.rsqrt(ms + EPS) * sub_ref[...] * (1.0 - lam_init))
        oa_ref[...] = jnp.concatenate(outs, axis=1).astype(oa_ref.dtype)
        p_self = bf(ssf_ref[...] / lfs_ref[...])
        of = accf_ref[...] + p_self * bf(vfn_ref[...])
        seg8 = lax.broadcasted_iota(I32, (8, W), 1) // HEAD_DIM
        of_ref[...] = jnp.sum(jnp.where(seg8 == rowi, of, 0.0), axis=0, keepdims=True).astype(of_ref.dtype)


PAGE = 128


def _decode(layer, page_table, qa, ka_new, va_new, qf, kf_new, vf_new, lf_new_col, lam_params, subln,
            ck_a, cv_a, ck_f, cv_f, clf_t, lam_init, *, pages_per_step):
    nb, n_pages = page_table.shape
    P = pages_per_step
    n_steps = n_pages // P
    W = A_WIDTH

    def page_spec(pp, shape_tail, value_phase):
        def imap(b, j, pt):
            step = jnp.maximum(j - n_steps, 0) if value_phase else jnp.minimum(j, n_steps - 1)
            return (layer, pt[b, n_pages - 1 - (step * P + pp)]) + (0,) * len(shape_tail)
        return pl.BlockSpec((None, None) + shape_tail, imap)

    rowspec = pl.BlockSpec((None, 1, W), lambda b, j, pt: (b, 0, 0))
    in_specs = [rowspec, rowspec, rowspec, rowspec, rowspec, rowspec,
                pl.BlockSpec((None, 8, 1), lambda b, j, pt: (b, 0, 0)),
                pl.BlockSpec((4, HEAD_DIM), lambda b, j, pt: (0, 0)),
                pl.BlockSpec((1, LANES), lambda b, j, pt: (0, 0))]
    args = [qa, ka_new, va_new, qf, kf_new, vf_new, lf_new_col, lam_params, subln.reshape(1, LANES)]
    for arr, tail, value_phase in ((ck_a, (W, PAGE), False), (ck_f, (W, PAGE), False),
                                   (clf_t, (F_HEADS, PAGE), False), (cv_a, (PAGE, A_HEADS, 2 * HEAD_DIM), True),
                                   (cv_f, (W, PAGE), True)):
        for pp in range(P):
            in_specs.append(page_spec(pp, tail, value_phase))
            args.append(arr)
    grid_spec = pltpu.PrefetchScalarGridSpec(
        num_scalar_prefetch=1,
        grid=(nb, 2 * n_steps),
        in_specs=in_specs,
        out_specs=[rowspec, rowspec],
        scratch_shapes=[pltpu.VMEM((16, W), BF16), pltpu.VMEM((8, W), BF16),
                        pltpu.VMEM((n_pages, 16, PAGE), F32), pltpu.VMEM((n_pages, 8, PAGE), F32),
                        pltpu.VMEM((16, 1), F32), pltpu.VMEM((8, 1), F32),
                        pltpu.VMEM((16, 1), F32), pltpu.VMEM((8, 1), F32),
                        pltpu.VMEM((8, W), F32), pltpu.VMEM((8, W), F32), pltpu.VMEM((8, 1), F32),
                        pltpu.VMEM((16, 1), F32), pltpu.VMEM((8, 1), F32)],
    )
    oa, of = pl.pallas_call(
        functools.partial(_decode_kernel, lam_init=lam_init, pages_per_step=P, n_steps=n_steps),
        grid_spec=grid_spec,
        out_shape=[jax.ShapeDtypeStruct((nb, 1, W), BF16), jax.ShapeDtypeStruct((nb, 1, W), BF16)],
        compiler_params=_cparams(("arbitrary", "arbitrary")),
        name="decode_attn",
    )(page_table, *args)
    return oa.reshape(nb, W), of.reshape(nb, W)


def _merge_kernel(x_ref, sh_ref, sc_ref, g1_ref, nw_ref, oa_ref, of_ref, os_ref, wg_ref, wa_ref, wf_ref, ws_ref,
                  wo_ref, o_ref):
    x = x_ref[...]
    d = x.shape[1]
    h = _norm_mod(x, nw_ref[...], sc_ref[...], sh_ref[...]).astype(BF16)
    m = None
    for j, (b_ref, w_ref) in enumerate(((oa_ref, wa_ref), (of_ref, wf_ref), (os_ref, ws_ref))):
        g = _sigmoid(_dot(h, wg_ref[:, j * d:(j + 1) * d]))
        t = g * _dot(b_ref[...], w_ref[...])
        m = t if m is None else m + t
    out = _dot(m.astype(BF16), wo_ref[...])
    o_ref[...] = x + g1_ref[...] * out


def _merge(x, shift, scale, gate, nw, oa, of, os_, wg, wa, wf, ws, wo, *, tm, tiles_per_batch, per_row_mod):
    n, d = x.shape
    if per_row_mod:
        mod_spec = pl.BlockSpec((tm, d), lambda i: (i, 0))
    else:
        mod_spec = pl.BlockSpec((None, 1, d), lambda i: (i // tiles_per_batch, 0, 0))
    row = lambda w: pl.BlockSpec((tm, w), lambda i: (i, 0))
    full = lambda a: pl.BlockSpec(a.shape, lambda i: (0,) * a.ndim)
    return pl.pallas_call(
        _merge_kernel,
        grid=(n // tm,),
        in_specs=[row(d), mod_spec, mod_spec, mod_spec, pl.BlockSpec((1, d), lambda i: (0, 0)),
                  row(A_WIDTH), row(F_WIDTH), row(S_WIDTH), full(wg), full(wa), full(wf), full(ws), full(wo)],
        out_specs=row(d),
        out_shape=jax.ShapeDtypeStruct((n, d), F32),
        compiler_params=_cparams(("arbitrary",)),
        name="merge",
    )(x, shift, scale, gate, nw, oa, of, os_, wg, wa, wf, ws, wo)


def _route_kernel(x_ref, sh_ref, sc_ref, nw_ref, rw_ref, rb_ref, h_ref, idx_ref, wt_ref, rank_ref, cnt_ref, run_ref):
    h = _norm_mod(x_ref[...], nw_ref[...], sc_ref[...], sh_ref[...])
    h_ref[...] = h
    tm = h.shape[0]
    logits = _dot_nt(rw_ref[...].astype(BF16), h.astype(BF16))
    scores = _sigmoid(logits)
    sel = scores + rb_ref[...]
    per = N_EXPERTS // N_EXPERT_GROUPS
    ei = lax.broadcasted_iota(I32, (N_EXPERTS, tm), 0).astype(F32)
    gid = jnp.floor(ei * (1.0 / per))
    BIG = 1e9
    gs_rows = []
    li = lax.broadcasted_iota(I32, (per, tm), 0).astype(F32)
    for g in range(N_EXPERT_GROUPS):
        sg = sel[g * per:(g + 1) * per, :]
        m1 = jnp.max(sg, axis=0, keepdims=True)
        i1 = jnp.min(jnp.where(sg == m1, li, BIG), axis=0, keepdims=True)
        m2 = jnp.max(jnp.where(li == i1, -jnp.inf, sg), axis=0, keepdims=True)
        gs_rows.append(m1 + m2)
    gs = jnp.concatenate(gs_rows, axis=0)
    gi = lax.broadcasted_iota(I32, (N_EXPERT_GROUPS, tm), 0).astype(F32)
    cur = jnp.full((N_EXPERTS, tm), NEG, F32)
    for _ in range(TOPK_GROUPS):
        m = jnp.max(gs, axis=0, keepdims=True)
        pick = jnp.min(jnp.where(gs == m, gi, BIG), axis=0, keepdims=True)
        gs = jnp.where(gi == pick, -jnp.inf, gs)
        cur = jnp.where(gid == pick, sel, cur)
    idx_rows, w_rows = [], []
    for _ in range(TOP_K):
        m = jnp.max(cur, axis=0, keepdims=True)
        pick = jnp.min(jnp.where(cur == m, ei, BIG), axis=0, keepdims=True)
        hit = ei == pick
        w_rows.append(jnp.sum(jnp.where(hit, scores, 0.0), axis=0, keepdims=True))
        idx_rows.append(pick)
        cur = jnp.where(hit, -jnp.inf, cur)
    w = jnp.concatenate(w_rows, axis=0)
    idx_ref[...] = jnp.concatenate(idx_rows, axis=0).astype(I32)
    wt_ref[...] = w / jnp.sum(w, axis=0, keepdims=True) * ROUTED_SCALE

    @pl.when(pl.program_id(0) == 0)
    def _():
        run_ref[...] = jnp.zeros_like(run_ref)

    chosen = jnp.zeros((N_EXPERTS, tm), F32)
    for pick in idx_rows:
        chosen = chosen + jnp.where(ei == pick, 1.0, 0.0)
    tr = lax.broadcasted_iota(I32, (tm, tm), 0)
    tcol = lax.broadcasted_iota(I32, (tm, tm), 1)
    before = jnp.where(tr < tcol, 1.0, 0.0).astype(BF16)
    base = _dot(chosen.astype(BF16), before) + run_ref[...]
    rank_ref[...] = jnp.concatenate(
        [jnp.sum(jnp.where(ei == pick, base, 0.0), axis=0, keepdims=True) for pick in idx_rows], axis=0).astype(I32)
    run_ref[...] = run_ref[...] + jnp.sum(chosen, axis=1, keepdims=True)
    cnt_ref[...] = run_ref[...]


def _route(x, shift, scale, nw, rw_t, rb_col, *, tm, tiles_per_batch, per_row_mod):
    n, d = x.shape
    if per_row_mod:
        mod_spec = pl.BlockSpec((tm, d), lambda i: (i, 0))
    else:
        mod_spec = pl.BlockSpec((None, 1, d), lambda i: (i // tiles_per_batch, 0, 0))
    row = lambda w: pl.BlockSpec((tm, w), lambda i: (i, 0))
    colspec = pl.BlockSpec((TOP_K, tm), lambda i: (0, i))
    return pl.pallas_call(
        _route_kernel,
        grid=(n // tm,),
        in_specs=[row(d), mod_spec, mod_spec, pl.BlockSpec((1, d), lambda i: (0, 0)),
                  pl.BlockSpec((N_EXPERTS, d), lambda i: (0, 0)), pl.BlockSpec((N_EXPERTS, 1), lambda i: (0, 0))],
        out_specs=[row(d), colspec, colspec, colspec, pl.BlockSpec((N_EXPERTS, 1), lambda i: (0, 0))],
        out_shape=[jax.ShapeDtypeStruct((n, d), F32), jax.ShapeDtypeStruct((TOP_K, n), I32),
                   jax.ShapeDtypeStruct((TOP_K, n), F32), jax.ShapeDtypeStruct((TOP_K, n), I32),
                   jax.ShapeDtypeStruct((N_EXPERTS, 1), F32)],
        scratch_shapes=[pltpu.VMEM((N_EXPERTS, 1), F32)],
        compiler_params=_cparams(("arbitrary",)),
        name="route",
    )(x, shift, scale, nw, rw_t, rb_col)


def _expert_kernel(be_ref, nu_ref, tok_hbm, x_hbm, wg_ref, wu_ref, wd_ref, y_ref, tok_smem, xbuf, isem, gsem, *, blk):
    i = pl.program_id(0)
    n_used = nu_ref[0]
    slot = i % 2

    def idx_copy(b, s):
        return pltpu.make_async_copy(tok_hbm.at[b], tok_smem.at[s], isem.at[s])

    @pl.when(i == 0)
    def _():
        idx_copy(0, 0).start()

    @pl.when(i < n_used)
    def _():
        idx_copy(i, slot).wait()

        @pl.when(i + 1 < n_used)
        def _():
            idx_copy(i + 1, 1 - slot).start()

        for r in range(blk):
            tok = tok_smem[slot, r]
            pltpu.make_async_copy(x_hbm.at[pl.ds(tok, 1), :], xbuf.at[slot, pl.ds(r, 1), :], gsem.at[slot]).start()

    @pl.when(jnp.logical_and(i >= 1, i - 1 < n_used))
    def _():
        ps = 1 - slot
        pltpu.make_async_copy(x_hbm.at[pl.ds(0, blk), :], xbuf.at[ps], gsem.at[ps]).wait()
        xb = xbuf[ps].astype(BF16)
        hg = _dot(xb, wg_ref[...])
        hu = _dot(xb, wu_ref[...])
        hh = (_silu(hg) * hu).astype(BF16)
        y_ref[...] = _dot(hh, wd_ref[...])

    @pl.when(jnp.logical_and(i >= 1, i - 1 >= n_used))
    def _():
        y_ref[...] = jnp.zeros_like(y_ref)


def _experts(x, slot_tok, block_e, n_used, wg, wu, wd, *, blk):
    n, d = x.shape
    n_blocks = slot_tok.shape[0]
    de = wg.shape[2]

    def wmap(i, be, nu):
        return (be[jnp.clip(i - 1, 0, n_blocks - 1)], 0, 0)

    def ymap(i, be, nu):
        return (jnp.maximum(i - 1, 0), 0)

    grid_spec = pltpu.PrefetchScalarGridSpec(
        num_scalar_prefetch=2,
        grid=(n_blocks + 1,),
        in_specs=[pl.BlockSpec(memory_space=pl.ANY), pl.BlockSpec(memory_space=pl.ANY),
                  pl.BlockSpec((None, d, de), wmap), pl.BlockSpec((None, d, de), wmap),
                  pl.BlockSpec((None, de, d), wmap)],
        out_specs=pl.BlockSpec((blk, d), ymap),
        scratch_shapes=[pltpu.SMEM((2, blk), I32), pltpu.VMEM((2, blk, d), F32),
                        pltpu.SemaphoreType.DMA((2,)), pltpu.SemaphoreType.DMA((2,))],
    )
    return pl.pallas_call(
        functools.partial(_expert_kernel, blk=blk),
        grid_spec=grid_spec,
        out_shape=jax.ShapeDtypeStruct((n_blocks * blk, d), F32),
        compiler_params=_cparams(("arbitrary",)),
        name="experts",
    )(block_e, n_used, slot_tok, x, wg, wu, wd)


def _combine_kernel(pos_hbm, y_hbm, x_ref, h_ref, g2_ref, wt_ref, sg_ref, su_ref, sd_ref, fin_ref, o_ref,
                    pos_smem, ybuf, isem, gsem, *, tc, n_tiles, final):
    i = pl.program_id(0)
    slot = i % 2

    def idx_copy(t, s):
        return pltpu.make_async_copy(pos_hbm.at[t], pos_smem.at[s], isem.at[s])

    @pl.when(i == 0)
    def _():
        idx_copy(0, 0).start()

    @pl.when(i < n_tiles)
    def _():
        idx_copy(i, slot).wait()

        @pl.when(i + 1 < n_tiles)
        def _():
            idx_copy(i + 1, 1 - slot).start()

        for t in range(tc):
            for k in range(TOP_K):
                p = pos_smem[slot, t * TOP_K + k]
                pltpu.make_async_copy(y_hbm.at[pl.ds(p, 1), :], ybuf.at[slot, k, pl.ds(t, 1), :],
                                      gsem.at[slot]).start()

    @pl.when(i >= 1)
    def _():
        ps = 1 - slot
        hb = h_ref[...].astype(BF16)
        sh = (_silu(_dot(hb, sg_ref[...])) * _dot(hb, su_ref[...])).astype(BF16)
        acc = _dot(sh, sd_ref[...])
        for k in range(TOP_K):
            pltpu.make_async_copy(y_hbm.at[pl.ds(0, tc), :], ybuf.at[ps, k], gsem.at[ps]).wait()
        w = wt_ref[...]
        for k in range(TOP_K):
            acc = acc + w[:, k:k + 1] * ybuf[ps, k]
        xo = x_ref[...] + g2_ref[...] * acc
        if final:
            ms = jnp.mean(xo * xo, axis=-1, keepdims=True)
            xo = xo * lax.rsqrt(ms + EPS) * fin_ref[...]
        o_ref[...] = xo


def _combine(pos_tiles, y_sorted, x, h, gate, wts, sg, su, sd, fin_w, *, tc, tiles_per_batch, per_row_mod, final):
    n, d = x.shape
    n_tiles = n // tc
    prev = lambda i: jnp.maximum(i - 1, 0)
    if per_row_mod:
        mod_spec = pl.BlockSpec((tc, d), lambda i: (prev(i), 0))
    else:
        mod_spec = pl.BlockSpec((None, 1, d), lambda i: (prev(i) // tiles_per_batch, 0, 0))
    row = lambda w: pl.BlockSpec((tc, w), lambda i: (prev(i), 0))
    full = lambda a: pl.BlockSpec(a.shape, lambda i: (0,) * a.ndim)
    return pl.pallas_call(
        functools.partial(_combine_kernel, tc=tc, n_tiles=n_tiles, final=final),
        grid=(n_tiles + 1,),
        in_specs=[pl.BlockSpec(memory_space=pl.ANY), pl.BlockSpec(memory_space=pl.ANY),
                  row(d), row(d), mod_spec, row(TOP_K), full(sg), full(su), full(sd),
                  pl.BlockSpec((1, d), lambda i: (0, 0))],
        out_specs=row(d),
        out_shape=jax.ShapeDtypeStruct((n, d), F32),
        scratch_shapes=[pltpu.SMEM((2, tc * TOP_K), I32), pltpu.VMEM((2, TOP_K, tc, d), F32),
                        pltpu.SemaphoreType.DMA((2,)), pltpu.SemaphoreType.DMA((2,))],
        compiler_params=_cparams(("arbitrary",)),
        name="combine",
    )(pos_tiles, y_sorted, x, h, gate, wts, sg, su, sd, fin_w)


def _dispatch_indices(idx, rank, counts, blk):
    n_tok = idx.shape[0]
    n_rows = n_tok * TOP_K
    n_blocks = -(-n_rows // blk) + N_EXPERTS
    padded = (counts + blk - 1) // blk * blk
    pad_end = jnp.cumsum(padded)
    pad_start = pad_end - padded
    eid = jnp.arange(N_EXPERTS, dtype=I32)
    first = jnp.sum(jnp.where(idx[..., None] == eid, pad_start, 0), axis=-1)
    pos = (first + rank).reshape(-1).astype(I32)
    slot_tok = jnp.zeros((n_blocks * blk,), I32).at[pos].set(jnp.arange(n_rows, dtype=I32) // TOP_K,
                                                              unique_indices=True)
    block_first = jnp.arange(n_blocks, dtype=I32) * blk
    block_e = jnp.minimum(jnp.sum((pad_end[None, :] <= block_first[:, None]).astype(I32), axis=1),
                          N_EXPERTS - 1).astype(I32)
    n_used = (pad_end[-1] // blk).astype(I32).reshape(1)
    return slot_tok.reshape(n_blocks, blk), pos, block_e, n_used


def _moe(x1, mod, nw, lw, fin_w, *, tm, tc, blk, tiles_per_batch_tm, tiles_per_batch_tc, per_row_mod, final):
    shift2, scale2, gate2 = mod
    h2, idx_t, wts_t, rank_t, cnt = _route(x1, shift2, scale2, nw, lw['router_w_t'], lw['router_b'], tm=tm,
                                           tiles_per_batch=tiles_per_batch_tm, per_row_mod=per_row_mod)
    idx = idx_t.T
    wts = wts_t.T
    slot_tok, pos, block_e, n_used = _dispatch_indices(idx, rank_t.T, cnt[:, 0].astype(I32), blk)
    y_sorted = _experts(h2, slot_tok, block_e, n_used, lw['exp_g'], lw['exp_u'], lw['exp_d'], blk=blk)
    n = x1.shape[0]
    pos_tiles = pos.reshape(n // tc, tc * TOP_K)
    return _combine(pos_tiles, y_sorted, x1, h2, gate2, wts, lw['sh_g'], lw['sh_u'], lw['sh_d'], fin_w,
                    tc=tc, tiles_per_batch=tiles_per_batch_tc, per_row_mod=per_row_mod, final=final)


def _pack_w_in(w_in_l):
    d = w_in_l.shape[0]
    cuts = [0]
    for wdt in (A_WIDTH, A_WIDTH, A_WIDTH, F_WIDTH, F_WIDTH, F_WIDTH, F_HEADS, S_WIDTH, CONV_DIM, S_HEADS,
                N_BRANCH * d):
        cuts.append(cuts[-1] + wdt)
    seg = lambda k: w_in_l[:, cuts[k]:cuts[k + 1]]
    small = jnp.concatenate([seg(6), seg(9), jnp.zeros((d, SMALL_W - F_HEADS - S_HEADS), w_in_l.dtype)], axis=1)
    pack = jnp.concatenate([seg(0), seg(1), seg(2), seg(3), seg(4), seg(5), seg(7), seg(8), small], axis=1)
    w_row = jnp.concatenate([seg(1), seg(2), seg(4), seg(7), seg(8), small], axis=1)
    w_t = jnp.concatenate([seg(k) for k in range(6)], axis=1).T
    return pack.astype(BF16), seg(10).astype(BF16), w_row.astype(BF16), w_t.astype(BF16)


def _rope_tables(pos):
    half = ROT_DIM // 2
    inv = ROPE_THETA ** (-jnp.arange(half, dtype=F32) / half)
    ang = pos.astype(F32)[:, None] * inv[None, :]
    cos, sin = jnp.cos(ang), jnp.sin(ang)
    n = pos.shape[0]
    one = jnp.ones((n, HEAD_DIM - ROT_DIM), F32)
    zero_h = jnp.zeros((n, half), F32)
    zero_r = jnp.zeros((n, HEAD_DIM - ROT_DIM), F32)
    rc = jnp.concatenate([cos, cos, one], axis=1)
    rn = jnp.concatenate([-sin, zero_h, zero_r], axis=1)
    rp = jnp.concatenate([zero_h, sin, zero_r], axis=1)
    rep = LANES // HEAD_DIM
    return tuple(jnp.tile(t, (1, rep)) for t in (rc, rn, rp))


def kernel(x_prompt, x_sample, cache_diff_k, cache_diff_v, cache_fox_k, cache_fox_v, cache_fox_logf, state_ssm, state_conv, page_table, c_prompt, c_sample, w_ada, b_ada, norm_mix, norm_ffn, w_in, diff_lambda, diff_subln, fox_b_f, ssm_conv_w, ssm_conv_b, ssm_dt_bias, ssm_a_log, ssm_d, ssm_norm, w_branch_a, w_branch_f, w_branch_s, w_out, router_w, router_bias, exp_w_gate, exp_w_up, exp_w_down, shared_w_gate, shared_w_up, shared_w_down, norm_final):
    nbp, seq, d = x_prompt.shape
    nbs, tdec, _ = x_sample.shape
    depth = w_in.shape[0]
    n_phys, page = cache_diff_k.shape[1], cache_diff_k.shape[2]
    n_pages = page_table.shape[1]
    assert tdec == 1 and page == PAGE
    n_p = nbp * seq
    n_s = nbs

    tm = min(256, seq)
    tq = min(256, seq)
    chunk = min(128, seq)
    tc_p = min(128, seq)
    blk_p = 256
    blk_s = 16
    pps = next(c for c in (4, 2, 1) if n_pages % c == 0)

    xp = x_prompt.reshape(n_p, d)
    xs = x_sample.reshape(n_s, d)
    ck_a = jnp.transpose(cache_diff_k, (0, 1, 3, 4, 5, 2)).reshape(depth, n_phys, A_WIDTH, page)
    ck_f = jnp.transpose(cache_fox_k, (0, 1, 3, 4, 2)).reshape(depth, n_phys, F_WIDTH, page)
    cv_f = jnp.transpose(cache_fox_v, (0, 1, 3, 4, 2)).reshape(depth, n_phys, F_WIDTH, page)
    cv_a = cache_diff_v
    clf_t = jnp.swapaxes(cache_fox_logf, 2, 3)

    tabs_p = _rope_tables(jnp.arange(seq))
    tabs_p_t = tuple(t.T for t in tabs_p)
    ck_sel = _ck_select()
    tabs_s = _rope_tables(jnp.full((n_s,), n_pages * page, I32))

    n_c = nbp + nbs
    n_c_pad = -(-n_c // 8) * 8
    c_all = jnp.concatenate([c_prompt, c_sample, jnp.zeros((n_c_pad - n_c, d), F32)], axis=0)

    st_p, st_s = [], []
    y_p = y_s = None
    fin_w = norm_final.reshape(1, d)
    for l in range(depth):
        lam_init = 0.8 - 0.6 * math.exp(-0.3 * l)
        w_pack, w_gates, w_row, w_t = _pack_w_in(w_in[l])
        bias_small = jnp.zeros((1, SMALL_W), F32).at[0, :F_HEADS].set(fox_b_f[l])
        lw = {'router_w_t': router_w[l].T, 'router_b': router_bias[l].reshape(N_EXPERTS, 1),
              'exp_g': exp_w_gate[l].astype(BF16), 'exp_u': exp_w_up[l].astype(BF16),
              'exp_d': exp_w_down[l].astype(BF16), 'sh_g': shared_w_gate[l].astype(BF16),
              'sh_u': shared_w_up[l].astype(BF16), 'sh_d': shared_w_down[l].astype(BF16)}
        wa, wf, ws = (w_branch_a[l].astype(BF16), w_branch_f[l].astype(BF16), w_branch_s[l].astype(BF16))
        wo = w_out[l].astype(BF16)
        sp = {'ssm_conv_w': ssm_conv_w[l], 'ssm_conv_b': ssm_conv_b[l], 'ssm_dt_bias': ssm_dt_bias[l],
              'ssm_a_log': ssm_a_log[l], 'ssm_d': ssm_d[l], 'ssm_norm': ssm_norm[l]}
        nm = norm_mix[l].reshape(1, d)
        nf = norm_ffn[l].reshape(1, d)
        final = l == depth - 1

        mod = _ada(c_all, w_ada[l], b_ada[l])
        mod6 = [mod[:, k * d:(k + 1) * d] for k in range(6)]
        mod_p = [m[:nbp].reshape(nbp, 1, d) for m in mod6]
        mod_s = [m[nbp:nbp + nbs] for m in mod6]

        (qat, kab, kat, va, vat, qft, kfb, ckf, kft, vft32, vft, sz, sxbc, small, lf, cum) = _proj_t(
            xp, mod_p[0], mod_p[1], nm, w_row, w_t, bias_small, tabs_p, tabs_p_t, ck_sel, nb=nbp, seq=seq, tm=tm)
        o_a = _attn(qat, kab, None, vat, diff_lambda[l], diff_subln[l], lam_init, fox=False, nb=nbp, seq=seq,
                    tq=tq)
        o_f = _attn(qft, kfb, ckf, vft, None, None, 0.0, fox=True, nb=nbp, seq=seq, tq=tq)
        o_s, h_fin, conv_p = _ssd_prompt(sxbc, sz, small, sp, nb=nbp, seq=seq, chunk=chunk)
        x1 = _merge(xp, mod_p[0], mod_p[1], mod_p[2], nm, o_a, o_f, o_s, w_gates, wa, wf, ws, wo, tm=tm,
                    tiles_per_batch=seq // tm, per_row_mod=False)
        xp = _moe(x1, mod_p[3:6], nf, lw, fin_w, tm=tm, tc=tc_p, blk=blk_p, tiles_per_batch_tm=seq // tm,
                  tiles_per_batch_tc=seq // tc_p, per_row_mod=False, final=final)
        st_p.append((kat, va.reshape(nbp, seq, A_HEADS, 2 * HEAD_DIM), kft, vft32,
                     lf[:, :F_HEADS].reshape(nbp, seq, F_HEADS), h_fin, conv_p))

        (qa, ka, kab, va, vab, qf, kf, kfb, vf, vfb, sz, sxbc, small, lf, cum) = _proj(
            xs, mod_s[0], mod_s[1], nm, w_pack, bias_small, tabs_s, tm=n_s, tiles_per_batch=1, per_row_mod=True)
        lf_new_col = lf[:, :F_HEADS].reshape(n_s, F_HEADS, 1)
        r3 = lambda a: a.reshape(n_s, 1, a.shape[1])
        o_a, o_f = _decode(l, page_table, r3(qa), r3(ka), r3(va), r3(qf), r3(kf), r3(vf), lf_new_col,
                           diff_lambda[l], diff_subln[l], ck_a, cv_a, ck_f, cv_f, clf_t, lam_init,
                           pages_per_step=pps)
        o_s, h_new, conv_s = _ssd_sample(sxbc, state_conv[l], sz, small, state_ssm[l], sp)
        x1 = _merge(xs, mod_s[0], mod_s[1], mod_s[2], nm, o_a, o_f, o_s, w_gates, wa, wf, ws, wo, tm=n_s,
                    tiles_per_batch=1, per_row_mod=True)
        xs = _moe(x1, mod_s[3:6], nf, lw, fin_w, tm=n_s, tc=n_s, blk=blk_s, tiles_per_batch_tm=1,
                  tiles_per_batch_tc=1, per_row_mod=True, final=final)
        st_s.append((ka.reshape(n_s, 1, A_HEADS, 2, HEAD_DIM), va.reshape(n_s, 1, A_HEADS, 2 * HEAD_DIM),
                     kf.reshape(n_s, 1, F_HEADS, HEAD_DIM), vf.reshape(n_s, 1, F_HEADS, HEAD_DIM),
                     lf[:, :F_HEADS].reshape(n_s, 1, F_HEADS), h_new, conv_s))

    y_p = xp.reshape(nbp, seq, d)
    y_s = xs.reshape(nbs, 1, d)
    stk = lambda states, i: jnp.stack([s[i] for s in states])

    def from_feature_major(a, inner):
        a = a.reshape((depth, nbp) + inner + (seq,))
        return jnp.moveaxis(a, -1, 2)

    return (y_p, y_s,
            from_feature_major(stk(st_p, 0), (A_HEADS, 2, HEAD_DIM)), stk(st_s, 0), stk(st_p, 1), stk(st_s, 1),
            from_feature_major(stk(st_p, 2), (F_HEADS, HEAD_DIM)), stk(st_s, 2),
            from_feature_major(stk(st_p, 3), (F_HEADS, HEAD_DIM)), stk(st_s, 3),
            stk(st_p, 4), stk(st_s, 4), stk(st_p, 5), stk(st_s, 5),
            stk(st_p, 6), stk(st_s, 6))
```
